```python
import math
import jax, jax.numpy as jnp
from jax import lax
import numpy as np

D_MODEL = 1024
BATCH = 2
SEQ = 8192
DEPTH = 2

CHUNK = 64
Q_BLOCK = 128
EPS = 1e-6

GLA_HEADS = 4
GLA_DK = 64
GLA_DV = 128
GLA_RANK = 16
GLA_TAU = 16.0
DSA_HEADS = 4
DSA_HD = 128
IDX_HEADS = 8
IDX_HD = 64
TOPK_MAX = 256
DIFF_HEADS = 4
DIFF_HD = 64

BR_W = 512
N_BRANCH = 3

SIZES = [
    GLA_HEADS * GLA_DK, GLA_HEADS * GLA_DK, GLA_HEADS * GLA_DV, GLA_RANK, BR_W,
    DSA_HEADS * DSA_HD, DSA_HEADS * DSA_HD, DSA_HEADS * DSA_HD,
    IDX_HEADS * IDX_HD, IDX_HD, IDX_HEADS, BR_W,
    DIFF_HEADS * 2 * DIFF_HD, DIFF_HEADS * 2 * DIFF_HD, DIFF_HEADS * 2 * DIFF_HD, BR_W,
    N_BRANCH * D_MODEL,
]
N_IN = int(sum(SIZES))
SPLIT_AT = [int(s) for s in np.cumsum(SIZES)[:-1]]

kernel_name = "hybrid_gla_dsa_diff_chunk_causal"

F32 = jnp.float32


def _rms(x, g=None):
    xf = x.astype(F32)
    y = xf * lax.rsqrt(jnp.mean(xf * xf, axis=-1, keepdims=True) + EPS)
    if g is not None:
        y = y * g.astype(F32)
    return y.astype(x.dtype)


def _chunk_visible(start, n_q, n_k):
    pos_q = start + jnp.arange(n_q)
    vis_end = (pos_q // CHUNK + 1) * CHUNK
    return jnp.arange(n_k)[None, :] < vis_end[:, None]


def _to_blocks(a, nb):
    return a.reshape((a.shape[0], nb, Q_BLOCK) + a.shape[2:]).swapaxes(0, 1)


def _gla(q, k, v, a_lr, wa2, ba, norm_g):
    B, T, _ = q.shape
    nc = T // CHUNK
    shp = (B, nc, CHUNK, GLA_HEADS)
    q = q.astype(F32).reshape(shp + (GLA_DK,)) * GLA_DK ** -0.5
    k = k.astype(F32).reshape(shp + (GLA_DK,))
    v = v.astype(F32).reshape(shp + (GLA_DV,))
    log_a = jax.nn.log_sigmoid((a_lr @ wa2 + ba).astype(F32)) / GLA_TAU
    log_a = log_a.reshape(shp + (GLA_DK,))
    cum = jnp.cumsum(log_a, axis=2)
    total = cum[:, :, -1]
    k_dec = k * jnp.exp(total[:, :, None] - cum)
    u = jnp.einsum('bnchk,bnchv->nbhkv', k_dec, v)
    decay = jnp.exp(total).transpose(1, 0, 2, 3)

    def step(S, inp):
        a, uc = inp
        S = a[..., None] * S + uc
        return S, S

    S0 = jnp.zeros((B, GLA_HEADS, GLA_DK, GLA_DV), F32)
    _, states = lax.scan(step, S0, (decay, u))
    o = jnp.einsum('bnchk,nbhkv->bnchv', q, states)
    o = _rms(o, norm_g)
    return o.reshape(B, T, GLA_HEADS * GLA_DV)


def _dsa(q, k, v, iq, ik, iw, qn_g, kn_g):
    B, T, _ = q.shape
    nb = T // Q_BLOCK
    topk = min(TOPK_MAX, T // 4)
    q = _rms(q.reshape(B, T, DSA_HEADS, DSA_HD), qn_g)
    k = _rms(k.reshape(B, T, DSA_HEADS, DSA_HD), kn_g)
    v = v.reshape(B, T, DSA_HEADS, DSA_HD)
    iq = iq.reshape(B, T, IDX_HEADS, IDX_HD)
    iw = iw.astype(F32) * IDX_HEADS ** -0.5
    starts = jnp.arange(nb) * Q_BLOCK

    def one(inp):
        qb, iqb, iwb, start = inp
        allowed = _chunk_visible(start, Q_BLOCK, T)
        dots = jnp.einsum('bqhd,bsd->bqhs', iqb, ik, preferred_element_type=F32) * IDX_HD ** -0.5
        score = jnp.einsum('bqhs,bqh->bqs', jax.nn.relu(dots), iwb)
        score = jnp.where(allowed[None], score, -jnp.inf)
        top_val, top_idx = lax.top_k(score, topk)
        valid = jnp.isfinite(top_val)
        k_sel = jax.vmap(lambda kk, ii: kk[ii])(k, top_idx)
        v_sel = jax.vmap(lambda vv, ii: vv[ii])(v, top_idx)
        logits = jnp.einsum('bqhd,bqshd->bhqs', qb, k_sel, preferred_element_type=F32) * DSA_HD ** -0.5
        logits = jnp.where(valid[:, None], logits, -jnp.inf)
        p = jax.nn.softmax(logits, axis=-1).astype(v.dtype)
        return jnp.einsum('bhqs,bqshd->bqhd', p, v_sel)

    out = lax.map(one, (_to_blocks(q, nb), _to_blocks(iq, nb), _to_blocks(iw, nb), starts))
    return out.swapaxes(0, 1).reshape(B, T, DSA_HEADS * DSA_HD)


def _diff(q, k, v, qn_g, kn_g, lq1, lk1, lq2, lk2, lambda_init):
    B, T, _ = q.shape
    nb = T // Q_BLOCK
    q = _rms(q.reshape(B, T, DIFF_HEADS, 2, DIFF_HD), qn_g)
    k = _rms(k.reshape(B, T, DIFF_HEADS, 2, DIFF_HD), kn_g)
    v = v.reshape(B, T, DIFF_HEADS, 2 * DIFF_HD)
    lam = (jnp.exp(jnp.sum(lq1.astype(F32) * lk1.astype(F32)))
           - jnp.exp(jnp.sum(lq2.astype(F32) * lk2.astype(F32))) + lambda_init)
    starts = jnp.arange(nb) * Q_BLOCK

    def one(inp):
        qb, start = inp
        allowed = _chunk_visible(start, Q_BLOCK, T)
        logits = jnp.einsum('bqhcd,bshcd->bhcqs', qb, k, preferred_element_type=F32) * DIFF_HD ** -0.5
        logits = jnp.where(allowed, logits, -jnp.inf)
        p = jax.nn.softmax(logits, axis=-1)
        attn = (p[:, :, 0] - lam * p[:, :, 1]).astype(v.dtype)
        return jnp.einsum('bhqs,bshd->bqhd', attn, v)

    out = lax.map(one, (_to_blocks(q, nb), starts)).swapaxes(0, 1)
    out = _rms(out) * (1.0 - lambda_init)
    return out.reshape(B, T, DIFF_HEADS * 2 * DIFF_HD)


def setup_inputs(seed: int = 0) -> dict:
    key = jax.random.key(seed)
    ks = jax.random.split(key, 16)
    nrm = jax.random.normal
    L = DEPTH
    return {
        "x": nrm(ks[0], (BATCH, SEQ, D_MODEL), F32),
        "norm_g": 1.0 + 0.02 * nrm(ks[1], (L, D_MODEL), F32),
        "w_in": nrm(ks[2], (L, D_MODEL, N_IN), F32) * D_MODEL ** -0.5,
        "gla_wa2": nrm(ks[3], (L, GLA_RANK, GLA_HEADS * GLA_DK), F32) * GLA_RANK ** -0.5,
        "gla_ba": 1.0 + 0.1 * nrm(ks[4], (L, GLA_HEADS * GLA_DK), F32),
        "gla_norm_g": 1.0 + 0.02 * nrm(ks[5], (L, GLA_DV), F32),
        "dsa_qn_g": 1.0 + 0.02 * nrm(ks[6], (L, DSA_HD), F32),
        "dsa_kn_g": 1.0 + 0.02 * nrm(ks[7], (L, DSA_HD), F32),
        "diff_qn_g": 1.0 + 0.02 * nrm(ks[8], (L, DIFF_HD), F32),
        "diff_kn_g": 1.0 + 0.02 * nrm(ks[9], (L, DIFF_HD), F32),
        "diff_lq1": 0.1 * nrm(ks[10], (L, DIFF_HD), F32),
        "diff_lk1": 0.1 * nrm(ks[11], (L, DIFF_HD), F32),
        "diff_lq2": 0.1 * nrm(ks[12], (L, DIFF_HD), F32),
        "diff_lk2": 0.1 * nrm(ks[13], (L, DIFF_HD), F32),
        "w_br": nrm(ks[14], (L, N_BRANCH, BR_W, D_MODEL), F32) * BR_W ** -0.5,
        "w_out": nrm(ks[15], (L, D_MODEL, D_MODEL), F32) * D_MODEL ** -0.5,
    }


def reference(x, norm_g, w_in, gla_wa2, gla_ba, gla_norm_g, dsa_qn_g, dsa_kn_g, diff_qn_g, diff_kn_g,
              diff_lq1, diff_lk1, diff_lq2, diff_lk2, w_br, w_out):
    B, T, _ = x.shape
    for l in range(DEPTH):
        h = _rms(x, norm_g[l])
        proj = h @ w_in[l]
        (gq, gk, gv, ga, gz, bq, bk, bv, iq, ik, iw, bz,
         cq, ck, cv, cz, gate) = jnp.split(proj, SPLIT_AT, axis=-1)
        lambda_init = 0.8 - 0.6 * math.exp(-0.3 * l)
        y_a = _gla(gq, gk, gv, ga, gla_wa2[l], gla_ba[l], gla_norm_g[l]).astype(x.dtype) * jax.nn.silu(gz)
        y_b = _dsa(bq, bk, bv, iq, ik, iw, dsa_qn_g[l], dsa_kn_g[l]) * jax.nn.silu(bz)
        y_c = _diff(cq, ck, cv, diff_qn_g[l], diff_kn_g[l], diff_lq1[l], diff_lk1[l],
                    diff_lq2[l], diff_lk2[l], lambda_init) * jax.nn.silu(cz)
        g = jax.nn.sigmoid(gate.reshape(B, T, N_BRANCH, D_MODEL))
        merged = (g[:, :, 0] * (y_a @ w_br[l, 0])
                  + g[:, :, 1] * (y_b @ w_br[l, 1])
                  + g[:, :, 2] * (y_c @ w_br[l, 2]))
        x = x + merged @ w_out[l]
    return x
```

```python
import functools
import math

import jax
import jax.numpy as jnp
from jax import lax
from jax.experimental import pallas as pl
from jax.experimental.pallas import tpu as pltpu

F32 = jnp.float32
BF16 = jnp.bfloat16
I32 = jnp.int32

D_MODEL = 1024
CHUNK = 64
CHUNK_SHIFT = 6
EPS = 1e-6
LANES = 128

GLA_HEADS, GLA_DK, GLA_DV, GLA_RANK, GLA_TAU = 4, 64, 128, 16, 16.0
DSA_HEADS, DSA_HD, IDX_HEADS, IDX_HD, TOPK_MAX = 4, 128, 8, 64, 256
DIFF_HEADS, DIFF_HD = 4, 64
BR_W, N_BRANCH = 512, 3

SIZES = [
    GLA_HEADS * GLA_DK, GLA_HEADS * GLA_DK, GLA_HEADS * GLA_DV, GLA_RANK, BR_W,
    DSA_HEADS * DSA_HD, DSA_HEADS * DSA_HD, DSA_HEADS * DSA_HD,
    IDX_HEADS * IDX_HD, IDX_HD, IDX_HEADS, BR_W,
    DIFF_HEADS * 2 * DIFF_HD, DIFF_HEADS * 2 * DIFF_HD, DIFF_HEADS * 2 * DIFF_HD, BR_W,
    N_BRANCH * D_MODEL,
]
OFFS = [0]
for _s in SIZES:
    OFFS.append(OFFS[-1] + _s)

INT_MIN = -(2 ** 31)
NEG = -1e30

PROJ_ROWS = 512
GLA_ROWS = 512
DSA_Q = 128
DSA_KB = 512
DSA_KB2 = 256
DIFF_Q = 256
VMEM_LIMIT = 56 * 1024 * 1024


def _cparams(sem):
    return pltpu.CompilerParams(dimension_semantics=sem, vmem_limit_bytes=VMEM_LIMIT)


def _sigmoid(x):
    return 1.0 / (1.0 + jnp.exp(-x))


def _silu(x):
    return x * _sigmoid(x)


def _rms_rows(x, g):
    ms = jnp.mean(x * x, axis=-1, keepdims=True)
    return (x * lax.rsqrt(ms + EPS)) * g


def _nt(a, b):
    return lax.dot_general(a, b, (((1,), (1,)), ((), ())), preferred_element_type=F32)


def _proj_gla_kernel(x_ref, g_ref, w_ref, q_ref, k_ref, v_ref, a_ref, z_ref):
    hb = _rms_rows(x_ref[...], g_ref[...]).astype(BF16)

    def seg(a, b):
        return jnp.dot(hb, w_ref[:, a:b], preferred_element_type=F32)

    q_ref[...] = seg(0, 256).astype(BF16)
    k_ref[...] = seg(256, 512).astype(BF16)
    v_ref[...] = seg(512, 1024).astype(BF16)
    a_ref[...] = seg(1024, 1152).astype(BF16)
    z_ref[...] = seg(1152, 1664)


def _head_rms(y, g):
    outs = []
    for h in range(4):
        yh = y[:, h * LANES:(h + 1) * LANES]
        ms = jnp.mean(yh * yh, axis=-1, keepdims=True)
        outs.append((yh * lax.rsqrt(ms + EPS)) * g)
    return outs


def _proj_dsa_kernel(x_ref, g_ref, w_ref, qg_ref, kg_ref,
                     q_ref, k_ref, v_ref, iq_ref, ik_ref, iw_ref, z_ref):
    hb = _rms_rows(x_ref[...], g_ref[...]).astype(BF16)

    def seg(a, b):
        return jnp.dot(hb, w_ref[:, a:b], preferred_element_type=F32)

    for h, qh in enumerate(_head_rms(seg(0, 512), qg_ref[...])):
        q_ref[:, h * LANES:(h + 1) * LANES] = qh.astype(BF16)
    for h, kh in enumerate(_head_rms(seg(512, 1024), kg_ref[...])):
        k_ref[:, h * LANES:(h + 1) * LANES] = kh.astype(BF16)
    v_ref[...] = seg(1024, 1536).astype(BF16)
    iq_ref[...] = seg(1536, 2048).astype(BF16)
    ik_ref[...] = seg(2048, 2176).astype(BF16)
    iw_ref[...] = seg(2176, 2304)
    z_ref[...] = seg(2304, 2816)


def _half_rms(y, g2):
    lo = lax.broadcasted_iota(I32, (y.shape[0], LANES), 1) < DIFF_HD
    outs = []
    for h in range(4):
        yh = y[:, h * LANES:(h + 1) * LANES]
        sq = yh * yh
        s_lo = jnp.sum(jnp.where(lo, sq, 0.0), axis=-1, keepdims=True)
        s_hi = jnp.sum(jnp.where(lo, 0.0, sq), axis=-1, keepdims=True)
        ms = jnp.where(lo, s_lo, s_hi) * (1.0 / DIFF_HD)
        outs.append((yh * lax.rsqrt(ms + EPS)) * g2)
    return outs, lo


def _proj_diff_kernel(x_ref, g_ref, w_ref, qg_ref, kg_ref,
                      q1_ref, q2_ref, k_ref, v_ref, z_ref):
    hb = _rms_rows(x_ref[...], g_ref[...]).astype(BF16)

    def seg(a, b):
        return jnp.dot(hb, w_ref[:, a:b], preferred_element_type=F32)

    qs, lo = _half_rms(seg(0, 512), qg_ref[...])
    for h, qh in enumerate(qs):
        qh = qh * (DIFF_HD ** -0.5)
        q1_ref[:, h * LANES:(h + 1) * LANES] = jnp.where(lo, qh, 0.0).astype(BF16)
        q2_ref[:, h * LANES:(h + 1) * LANES] = jnp.where(lo, 0.0, qh).astype(BF16)
    ks, _ = _half_rms(seg(512, 1024), kg_ref[...])
    for h, kh in enumerate(ks):
        k_ref[:, h * LANES:(h + 1) * LANES] = kh.astype(BF16)
    v_ref[...] = seg(1024, 1536).astype(BF16)
    z_ref[...] = seg(1536, 2048)


def _row_spec(width, rows=PROJ_ROWS):
    return pl.BlockSpec((rows, width), lambda i: (i, 0))


def _full_spec(shape):
    nd = len(shape)
    return pl.BlockSpec(shape, lambda i: (0,) * nd)


def _proj_call(kernel, name, x2, g, w, extras, outs):
    n = x2.shape[0]
    in_specs = [_row_spec(D_MODEL), _full_spec(g.shape), _full_spec(w.shape)]
    in_specs += [_full_spec(e.shape) for e in extras]
    return pl.pallas_call(
        kernel,
        grid=(n // PROJ_ROWS,),
        in_specs=in_specs,
        out_specs=[_row_spec(wd) for wd, _ in outs],
        out_shape=[jax.ShapeDtypeStruct((n, wd), dt) for wd, dt in outs],
        compiler_params=_cparams(("parallel",)),
        name=name,
    )(x2, g, w, *extras)


def _gla_kernel(q_ref, k_ref, v_ref, a_ref, z_ref, wa_ref, ba_ref, ng_ref, o_ref, st_ref):
    rows = q_ref.shape[0]
    nchunk = rows // CHUNK

    @pl.when(pl.program_id(1) == 0)
    def _():
        st_ref[...] = jnp.zeros_like(st_ref)

    pre = jnp.dot(a_ref[...], wa_ref[...], preferred_element_type=F32) + ba_ref[...]
    log_a = -(jnp.maximum(-pre, 0.0) + jnp.log1p(jnp.exp(-jnp.abs(pre)))) / GLA_TAU
    pos = lax.broadcasted_iota(I32, (rows, 1), 0) & (CHUNK - 1)
    cum = log_a
    s = 1
    while s < CHUNK:
        cum = cum + jnp.where(pos >= s, pltpu.roll(cum, s, axis=0), 0.0)
        s *= 2
    lo = lax.broadcasted_iota(I32, (CHUNK, LANES), 1) < GLA_DK

    for c in range(nchunk):
        r0 = c * CHUNK
        total = cum[r0 + CHUNK - 1:r0 + CHUNK, :]
        decay_to_end = jnp.exp(total - cum[r0:r0 + CHUNK, :])
        k_dec = (k_ref[r0:r0 + CHUNK, :].astype(F32) * decay_to_end).astype(BF16)
        a_tot = jnp.exp(total)
        q_c = q_ref[r0:r0 + CHUNK, :].astype(F32) * (GLA_DK ** -0.5)
        for h in range(GLA_HEADS):
            j = h // 2
            pair = slice(j * LANES, (j + 1) * LANES)
            v_h = v_ref[r0:r0 + CHUNK, h * GLA_DV:(h + 1) * GLA_DV]
            u_t = lax.dot_general(v_h, k_dec[:, pair], (((0,), (0,)), ((), ())),
                                  preferred_element_type=F32)
            st = st_ref[h] * a_tot[:, pair] + u_t
            st_ref[h] = st
            keep = lo if h % 2 == 0 else jnp.logical_not(lo)
            q_h = jnp.where(keep, q_c[:, pair], 0.0).astype(BF16)
            o = _nt(q_h, st.astype(BF16))
            ms = jnp.mean(o * o, axis=-1, keepdims=True)
            o = (o * lax.rsqrt(ms + EPS)) * ng_ref[...]
            zh = z_ref[r0:r0 + CHUNK, h * GLA_DV:(h + 1) * GLA_DV]
            o_ref[r0:r0 + CHUNK, h * GLA_DV:(h + 1) * GLA_DV] = (o * _silu(zh)).astype(BF16)


def _gla_call(q, k, v, a, z, wa, ba, ng, b, t):
    nb = t // GLA_ROWS

    def rs(width):
        return pl.BlockSpec((GLA_ROWS, width), lambda bi, i: (bi * nb + i, 0))

    def fs(shape):
        return pl.BlockSpec(shape, lambda bi, i: (0,) * len(shape))

    return pl.pallas_call(
        _gla_kernel,
        grid=(b, nb),
        in_specs=[rs(256), rs(256), rs(512), rs(128), rs(512),
                  fs(wa.shape), fs(ba.shape), fs(ng.shape)],
        out_specs=rs(512),
        out_shape=jax.ShapeDtypeStruct((b * t, BR_W), BF16),
        scratch_shapes=[pltpu.VMEM((GLA_HEADS, GLA_DV, LANES), F32)],
        compiler_params=_cparams(("parallel", "arbitrary")),
        name="gla_mixer",
    )(q, k, v, a, z, wa, ba, ng)


def _dsa_kernel(q_ref, iq_ref, iw_ref, z_ref, k_ref, v_ref, ik_ref, o_ref,
                keys_ref, acc_ref, m_ref, l_ref, *, topk, idx_bits):
    nq = q_ref.shape[0]
    kb_sz = keys_ref.shape[2]
    ncol = kb_sz // LANES
    qi = pl.program_id(1)
    q0 = qi * nq
    nkb = (q0 + nq + kb_sz - 1) >> (kb_sz.bit_length() - 1)

    lane = lax.broadcasted_iota(I32, (nq, LANES), 1)
    lo = lane < IDX_HD
    row_end = q0 + ((lax.broadcasted_iota(I32, (nq, 1), 0) >> CHUNK_SHIFT) + 1) * CHUNK

    iq = iq_ref[...].astype(F32)
    parts = []
    for j in range(IDX_HEADS // 2):
        pair = iq[:, j * LANES:(j + 1) * LANES]
        parts.append(jnp.where(lo, pair, 0.0).astype(BF16))
        parts.append(jnp.where(lo, 0.0, pair).astype(BF16))
    iq_all = jnp.concatenate(parts, axis=0)
    wc = iw_ref[...] * (IDX_HEADS ** -0.5 * IDX_HD ** -0.5)
    wcols = [jnp.broadcast_to(wc[:, h:h + 1], (nq, LANES)) for h in range(IDX_HEADS)]

    def score_body(kb, carry):
        kstart = pl.multiple_of(kb * kb_sz, kb_sz)
        d = _nt(iq_all, ik_ref[pl.ds(kstart, kb_sz), :])
        for c in range(ncol):
            cs = slice(c * LANES, (c + 1) * LANES)
            s = jnp.zeros((nq, LANES), F32)
            for h in range(IDX_HEADS):
                s = s + jnp.maximum(d[h * nq:(h + 1) * nq, cs], 0.0) * wcols[h]
            s = jnp.where(s == 0.0, 0.0, s)
            bits = lax.bitcast_convert_type(s, I32)
            key = bits ^ ((bits >> 31) & 0x7FFFFFFF)
            col = kstart + c * LANES + lane
            keys_ref[kb, :, cs] = jnp.where(col < row_end, key, INT_MIN)
        return carry

    lax.fori_loop(0, nkb, score_body, 0)


    def count(pred):
        def body(kb, part):
            for c in range(ncol):
                part = part + jnp.where(pred(kb, c), 1.0, 0.0)
            return part
        part = lax.fori_loop(0, nkb, body, jnp.zeros((nq, LANES), F32))
        return jnp.sum(part, axis=-1, keepdims=True)

    def blk(kb, c):
        return keys_ref[kb, :, c * LANES:(c + 1) * LANES]

    def bit_body(i, carry):
        res, cres = carry
        cand = res | lax.shift_left(jnp.int32(1), 31 - i)
        cand_b = jnp.broadcast_to(cand ^ INT_MIN, (nq, LANES))
        cnt = count(lambda kb, c: blk(kb, c) >= cand_b)
        take = cnt >= topk
        return jnp.where(take, cand, res), jnp.where(take, cnt, cres)

    res, cres = lax.fori_loop(0, 32, bit_body,
                              (jnp.zeros((nq, 1), I32), jnp.zeros((nq, 1), F32)))
    thr = jnp.maximum(res ^ INT_MIN, INT_MIN + 1)
    thr_b = jnp.broadcast_to(thr, (nq, LANES))

    excess = jnp.where(cres > topk, 1.0, 0.0)

    @pl.when(jnp.max(excess) > 0.0)
    def _():
        need = topk - count(lambda kb, c: blk(kb, c) > thr_b)

        def col_of(kb, c):
            return kb * kb_sz + c * LANES + lane

        def idx_body(i, p):
            cand = p | lax.shift_left(jnp.int32(1), idx_bits - 1 - i)
            cand_b = jnp.broadcast_to(cand, (nq, LANES))
            below = count(lambda kb, c: jnp.logical_and(blk(kb, c) == thr_b,
                                                        col_of(kb, c) < cand_b))
            return jnp.where(below <= need - 1.0, cand, p)

        last = lax.fori_loop(0, idx_bits, idx_body, jnp.zeros((nq, 1), I32))
        last_b = jnp.broadcast_to(last, (nq, LANES))
        excess_b = jnp.broadcast_to(excess, (nq, LANES)) > 0.0

        def demote_body(kb, carry):
            for c in range(ncol):
                kv = blk(kb, c)
                drop = jnp.logical_and(excess_b, jnp.logical_and(kv == thr_b,
                                                                 col_of(kb, c) > last_b))
                keys_ref[kb, :, c * LANES:(c + 1) * LANES] = jnp.where(drop, INT_MIN, kv)
            return carry

        lax.fori_loop(0, nkb, demote_body, 0)

    acc_ref[...] = jnp.zeros_like(acc_ref)
    m_ref[...] = jnp.full_like(m_ref, NEG)
    l_ref[...] = jnp.zeros_like(l_ref)
    kb2 = DSA_KB2
    scale = DSA_HD ** -0.5

    def att_body(kb, carry):
        kstart = pl.multiple_of(kb * kb_sz, kb_sz)
        for half in range(kb_sz // kb2):
            sel = keys_ref[kb, :, half * kb2:(half + 1) * kb2] >= jnp.broadcast_to(thr, (nq, kb2))
            rows = pl.ds(kstart + half * kb2, kb2)
            for h in range(DSA_HEADS):
                hs = slice(h * DSA_HD, (h + 1) * DSA_HD)
                s = _nt(q_ref[:, hs], k_ref[rows, hs]) * scale
                s = jnp.where(sel, s, NEG)
                m_old = m_ref[h]
                m_new = jnp.maximum(m_old, jnp.max(s, axis=-1, keepdims=True))
                alpha = jnp.exp(m_old - m_new)
                p = jnp.exp(s - m_new)
                l_ref[h] = alpha * l_ref[h] + jnp.sum(p, axis=-1, keepdims=True)
                acc_ref[h] = alpha * acc_ref[h] + jnp.dot(
                    p.astype(BF16), v_ref[rows, hs], preferred_element_type=F32)
                m_ref[h] = m_new
        return carry

    lax.fori_loop(0, nkb, att_body, 0)

    for h in range(DSA_HEADS):
        hs = slice(h * DSA_HD, (h + 1) * DSA_HD)
        o = acc_ref[h] / l_ref[h]
        o_ref[:, hs] = (o * _silu(z_ref[:, hs])).astype(BF16)


def _dsa_call(q, iq, iw, z, k, v, ik, b, t):
    nq = t // DSA_Q
    topk = min(TOPK_MAX, t // 4)
    idx_bits = max(1, (t - 1).bit_length())

    def qs(width):
        return pl.BlockSpec((DSA_Q, width), lambda bi, i: (bi * nq + i, 0))

    def ks(width):
        return pl.BlockSpec((t, width), lambda bi, i: (bi, 0))

    kernel = functools.partial(_dsa_kernel, topk=topk, idx_bits=idx_bits)
    return pl.pallas_call(
        kernel,
        grid=(b, nq),
        in_specs=[qs(512), qs(512), qs(128), qs(512), ks(512), ks(512), ks(128)],
        out_specs=qs(512),
        out_shape=jax.ShapeDtypeStruct((b * t, BR_W), BF16),
        scratch_shapes=[
            pltpu.VMEM((pl.cdiv(t, DSA_KB), DSA_Q, DSA_KB), I32),
            pltpu.VMEM((DSA_HEADS, DSA_Q, DSA_HD), F32),
            pltpu.VMEM((DSA_HEADS, DSA_Q, 1), F32),
            pltpu.VMEM((DSA_HEADS, DSA_Q, 1), F32),
        ],
        compiler_params=_cparams(("parallel", "arbitrary")),
        name="dsa_mixer",
    )(q, iq, iw, z, k, v, ik)


def _diff_kernel(q1_ref, q2_ref, z_ref, lq1_ref, lk1_ref, lq2_ref, lk2_ref, k_ref, v_ref,
                 o_ref, acc_ref, m_ref, l_ref, *, lambda_init):
    nq = q1_ref.shape[0]
    qi = pl.program_id(1)
    lam = (jnp.exp(jnp.sum(lq1_ref[...] * lk1_ref[...], axis=-1, keepdims=True))
           - jnp.exp(jnp.sum(lq2_ref[...] * lk2_ref[...], axis=-1, keepdims=True))
           + lambda_init)

    acc_ref[...] = jnp.zeros_like(acc_ref)
    m_ref[...] = jnp.full_like(m_ref, NEG)
    l_ref[...] = jnp.zeros_like(l_ref)

    r_end = ((lax.broadcasted_iota(I32, (nq, 1), 0) >> CHUNK_SHIFT) + 1) * CHUNK
    allowed = lax.broadcasted_iota(I32, (nq, nq), 1) < r_end

    def block(kb, masked):
        rows = pl.ds(pl.multiple_of(kb * nq, nq), nq)
        for h in range(DIFF_HEADS):
            hs = slice(h * LANES, (h + 1) * LANES)
            k_h = k_ref[rows, hs]
            v_h = v_ref[rows, hs]
            for c, q_ref in enumerate((q1_ref, q2_ref)):
                i = 2 * h + c
                s = _nt(q_ref[:, hs], k_h)
                if masked:
                    s = jnp.where(allowed, s, NEG)
                m_old = m_ref[i]
                m_new = jnp.maximum(m_old, jnp.max(s, axis=-1, keepdims=True))
                alpha = jnp.exp(m_old - m_new)
                p = jnp.exp(s - m_new)
                l_ref[i] = alpha * l_ref[i] + jnp.sum(p, axis=-1, keepdims=True)
                acc_ref[i] = alpha * acc_ref[i] + jnp.dot(
                    p.astype(BF16), v_h, preferred_element_type=F32)
                m_ref[i] = m_new

    def body(kb, carry):
        block(kb, False)
        return carry

    lax.fori_loop(0, qi, body, 0)
    block(qi, True)

    for h in range(DIFF_HEADS):
        hs = slice(h * LANES, (h + 1) * LANES)
        o = acc_ref[2 * h] / l_ref[2 * h] - lam * (acc_ref[2 * h + 1] / l_ref[2 * h + 1])
        ms = jnp.mean(o * o, axis=-1, keepdims=True)
        o = (o * lax.rsqrt(ms + EPS)) * (1.0 - lambda_init)
        o_ref[:, hs] = (o * _silu(z_ref[:, hs])).astype(BF16)


def _diff_call(q1, q2, z, lq1, lk1, lq2, lk2, k, v, b, t, lambda_init):
    nq = t // DIFF_Q

    def qs(width):
        return pl.BlockSpec((DIFF_Q, width), lambda bi, i: (bi * nq + i, 0))

    def ks(width):
        return pl.BlockSpec((t, width), lambda bi, i: (bi, 0))

    def fs(shape):
        return pl.BlockSpec(shape, lambda bi, i: (0,) * len(shape))

    kernel = functools.partial(_diff_kernel, lambda_init=lambda_init)
    return pl.pallas_call(
        kernel,
        grid=(b, nq),
        in_specs=[qs(512), qs(512), qs(512), fs(lq1.shape), fs(lk1.shape), fs(lq2.shape),
                  fs(lk2.shape), ks(512), ks(512)],
        out_specs=qs(512),
        out_shape=jax.ShapeDtypeStruct((b * t, BR_W), BF16),
        scratch_shapes=[
            pltpu.VMEM((2 * DIFF_HEADS, DIFF_Q, 2 * DIFF_HD), F32),
            pltpu.VMEM((2 * DIFF_HEADS, DIFF_Q, 1), F32),
            pltpu.VMEM((2 * DIFF_HEADS, DIFF_Q, 1), F32),
        ],
        compiler_params=_cparams(("parallel", "arbitrary")),
        name="diff_mixer",
    )(q1, q2, z, lq1, lk1, lq2, lk2, k, v)


def _out_kernel(x_ref, g_ref, wg_ref, ya_ref, yb_ref, yc_ref, wbr_ref, wo_ref, o_ref):
    x = x_ref[...]
    hb = _rms_rows(x, g_ref[...]).astype(BF16)
    merged = jnp.zeros(x.shape, F32)
    for i, y_ref in enumerate((ya_ref, yb_ref, yc_ref)):
        gate = _sigmoid(jnp.dot(hb, wg_ref[:, i * D_MODEL:(i + 1) * D_MODEL],
                                preferred_element_type=F32))
        merged = merged + gate * jnp.dot(y_ref[...], wbr_ref[i], preferred_element_type=F32)
    o_ref[...] = x + jnp.dot(merged.astype(BF16), wo_ref[...], preferred_element_type=F32)


def _out_call(x2, g, wg, ya, yb, yc, wbr, wo):
    n = x2.shape[0]
    return pl.pallas_call(
        _out_kernel,
        grid=(n // PROJ_ROWS,),
        in_specs=[_row_spec(D_MODEL), _full_spec(g.shape), _full_spec(wg.shape),
                  _row_spec(BR_W), _row_spec(BR_W), _row_spec(BR_W),
                  _full_spec(wbr.shape), _full_spec(wo.shape)],
        out_specs=_row_spec(D_MODEL),
        out_shape=jax.ShapeDtypeStruct((n, D_MODEL), F32),
        compiler_params=_cparams(("parallel",)),
        name="merge_out",
    )(x2, g, wg, ya, yb, yc, wbr, wo)


def _pad_cols(w, width):
    return jnp.pad(w, ((0, 0), (0, width - w.shape[1])))


def kernel(x, norm_g, w_in, gla_wa2, gla_ba, gla_norm_g, dsa_qn_g, dsa_kn_g, diff_qn_g,
           diff_kn_g, diff_lq1, diff_lk1, diff_lq2, diff_lk2, w_br, w_out):
    b, t, d = x.shape
    depth = w_in.shape[0]
    assert d == D_MODEL and t % max(GLA_ROWS, DSA_KB, DIFF_Q) == 0 and (b * t) % PROJ_ROWS == 0
    x2 = x.reshape(b * t, d)

    for l in range(depth):
        w = w_in[l]
        seg = [w[:, OFFS[i]:OFFS[i + 1]] for i in range(len(SIZES))]
        (gq, gk, gv, ga, gz, bq, bk, bv, iq, ik, iw, bz, cq, ck, cv, cz, gate) = seg
        w_gla = jnp.concatenate([gq, gk, gv, _pad_cols(ga, LANES), gz], axis=1).astype(BF16)
        w_dsa = jnp.concatenate([bq, bk, bv, iq, ik, ik, _pad_cols(iw, LANES), bz],
                                axis=1).astype(BF16)
        w_diff = jnp.concatenate([cq, ck, cv, cz], axis=1).astype(BF16)
        w_gate = gate.astype(BF16)
        g = norm_g[l].reshape(1, d)

        g_q, g_k, g_v, g_a, g_z = _proj_call(
            _proj_gla_kernel, "proj_gla", x2, g, w_gla, [],
            [(256, BF16), (256, BF16), (512, BF16), (128, BF16), (512, F32)])
        b_q, b_k, b_v, i_q, i_k, i_w, b_z = _proj_call(
            _proj_dsa_kernel, "proj_dsa", x2, g, w_dsa,
            [dsa_qn_g[l].reshape(1, DSA_HD), dsa_kn_g[l].reshape(1, DSA_HD)],
            [(512, BF16), (512, BF16), (512, BF16), (512, BF16), (128, BF16), (128, F32),
             (512, F32)])
        c_q1, c_q2, c_k, c_v, c_z = _proj_call(
            _proj_diff_kernel, "proj_diff", x2, g, w_diff,
            [jnp.tile(diff_qn_g[l], 2).reshape(1, LANES),
             jnp.tile(diff_kn_g[l], 2).reshape(1, LANES)],
            [(512, BF16), (512, BF16), (512, BF16), (512, BF16), (512, F32)])

        wa = jnp.pad(gla_wa2[l], ((0, LANES - GLA_RANK), (0, 0))).astype(BF16)
        y_a = _gla_call(g_q, g_k, g_v, g_a, g_z, wa, gla_ba[l].reshape(1, -1),
                        gla_norm_g[l].reshape(1, GLA_DV), b, t)
        y_b = _dsa_call(b_q, i_q, i_w, b_z, b_k, b_v, i_k, b, t)
        lambda_init = 0.8 - 0.6 * math.exp(-0.3 * l)
        y_c = _diff_call(c_q1, c_q2, c_z, diff_lq1[l].reshape(1, -1), diff_lk1[l].reshape(1, -1),
                         diff_lq2[l].reshape(1, -1), diff_lk2[l].reshape(1, -1), c_k, c_v,
                         b, t, lambda_init)
        x2 = _out_call(x2, g, w_gate, y_a, y_b, y_c, w_br[l].astype(BF16),
                       w_out[l].astype(BF16))
    return x2.reshape(b, t, d)
```

```python
import functools
import math

import jax
import jax.numpy as jnp
from jax import lax
from jax.experimental import pallas as pl
from jax.experimental.pallas import tpu as pltpu

F32 = jnp.float32
BF16 = jnp.bfloat16
I32 = jnp.int32

D_MODEL = 1024
CHUNK = 64
CHUNK_SHIFT = 6
EPS = 1e-6
LANES = 128
SUBLANES = 8

GLA_HEADS, GLA_DK, GLA_DV, GLA_RANK, GLA_TAU = 4, 64, 128, 16, 16.0
DSA_HEADS, DSA_HD, IDX_HEADS, IDX_HD, TOPK_MAX = 4, 128, 8, 64, 256
DIFF_HEADS, DIFF_HD = 4, 64
BR_W, N_BRANCH = 512, 3

SIZES = [
    GLA_HEADS * GLA_DK, GLA_HEADS * GLA_DK, GLA_HEADS * GLA_DV, GLA_RANK, BR_W,
    DSA_HEADS * DSA_HD, DSA_HEADS * DSA_HD, DSA_HEADS * DSA_HD,
    IDX_HEADS * IDX_HD, IDX_HD, IDX_HEADS, BR_W,
    DIFF_HEADS * 2 * DIFF_HD, DIFF_HEADS * 2 * DIFF_HD, DIFF_HEADS * 2 * DIFF_HD, BR_W,
    N_BRANCH * D_MODEL,
]
OFFS = [0]
for _s in SIZES:
    OFFS.append(OFFS[-1] + _s)

INT_MIN = -(2 ** 31)
NEG = -1e30
LOG2E = math.log2(math.e)

PROJ_ROWS = 512
GLA_ROWS = 512
TBLK = 256
DSA_KB = 512
SUB = 64
VMEM_LIMIT = 56 * 1024 * 1024


def _cparams(sem):
    return pltpu.CompilerParams(dimension_semantics=sem, vmem_limit_bytes=VMEM_LIMIT)


def _sigmoid(x):
    return 1.0 / (1.0 + jnp.exp(-x))


def _silu(x):
    return x * _sigmoid(x)


def _rms_rows(x, g):
    ms = jnp.mean(x * x, axis=-1, keepdims=True)
    return (x * lax.rsqrt(ms + EPS)) * g


def _nt(a, b):
    return lax.dot_general(a, b, (((1,), (1,)), ((), ())), preferred_element_type=F32)


def _fold(x, op):
    return op(x.reshape(x.shape[0] // SUBLANES, SUBLANES, x.shape[1]), axis=0)


def _proj_gla_kernel(x_ref, g_ref, w_ref, q_ref, k_ref, v_ref, a_ref, z_ref):
    hb = _rms_rows(x_ref[...], g_ref[...]).astype(BF16)

    def seg(a, b):
        return jnp.dot(hb, w_ref[:, a:b], preferred_element_type=F32)

    q_ref[...] = seg(0, 256).astype(BF16)
    k_ref[...] = seg(256, 512).astype(BF16)
    v_ref[...] = seg(512, 1024).astype(BF16)
    a_ref[...] = seg(1024, 1152).astype(BF16)
    z_ref[...] = seg(1152, 1664)


def _head_rms(y, g):
    outs = []
    for h in range(4):
        yh = y[:, h * LANES:(h + 1) * LANES]
        ms = jnp.mean(yh * yh, axis=-1, keepdims=True)
        outs.append((yh * lax.rsqrt(ms + EPS)) * g)
    return outs


def _proj_dsa_kernel(x_ref, g_ref, wt_ref, w_ref, qg_ref, kg_ref,
                     qt_ref, k_ref, vt_ref, iqt_ref, ik_ref, iwt_ref, z_ref):
    hb = _rms_rows(x_ref[...], g_ref[...]).astype(BF16)
    nblk = hb.shape[0] // TBLK

    def seg(a, b):
        return jnp.dot(hb, w_ref[:, a:b], preferred_element_type=F32)

    def seg_t(a, b):
        return _nt(wt_ref[a:b, :], hb)

    q_t = seg_t(0, 512)
    for h in range(DSA_HEADS):
        rs = slice(h * DSA_HD, (h + 1) * DSA_HD)
        xh = q_t[rs, :]
        ms = jnp.mean(xh * xh, axis=0, keepdims=True)
        xn = ((xh * lax.rsqrt(ms + EPS)) * qg_ref[...]).astype(BF16)
        for blk in range(nblk):
            qt_ref[blk, rs, :] = xn[:, blk * TBLK:(blk + 1) * TBLK]
    v_t = seg_t(512, 1024).astype(BF16)
    iq_t = seg_t(1024, 1536).astype(BF16)
    iw_t = seg_t(1536, 1552)
    for blk in range(nblk):
        cs = slice(blk * TBLK, (blk + 1) * TBLK)
        vt_ref[blk] = v_t[:, cs]
        iqt_ref[blk] = iq_t[:, cs]
        iwt_ref[blk] = iw_t[0:IDX_HEADS, cs]
    for h, kh in enumerate(_head_rms(seg(0, 512), kg_ref[...])):
        k_ref[:, h * LANES:(h + 1) * LANES] = kh.astype(BF16)
    ik_ref[...] = seg(512, 640).astype(BF16)
    z_ref[...] = seg(640, 1152)


def _half_rms(y, g2):
    lo = lax.broadcasted_iota(I32, (y.shape[0], LANES), 1) < DIFF_HD
    outs = []
    for h in range(4):
        yh = y[:, h * LANES:(h + 1) * LANES]
        sq = yh * yh
        s_lo = jnp.sum(jnp.where(lo, sq, 0.0), axis=-1, keepdims=True)
        s_hi = jnp.sum(jnp.where(lo, 0.0, sq), axis=-1, keepdims=True)
        ms = jnp.where(lo, s_lo, s_hi) * (1.0 / DIFF_HD)
        outs.append((yh * lax.rsqrt(ms + EPS)) * g2)
    return outs


def _proj_diff_kernel(x_ref, g_ref, wt_ref, w_ref, qg_ref, kg_ref,
                      q1_ref, q2_ref, k_ref, vt_ref, z_ref):
    hb = _rms_rows(x_ref[...], g_ref[...]).astype(BF16)
    nblk = hb.shape[0] // TBLK
    q_t = _nt(wt_ref[0:512, :], hb)
    v_t = _nt(wt_ref[512:1024, :], hb)
    zeros = jnp.zeros((DIFF_HD, TBLK), BF16)
    for j in range(2 * DIFF_HEADS):
        rs = slice(j * DIFF_HD, (j + 1) * DIFF_HD)
        xj = q_t[rs, :]
        ms = jnp.mean(xj * xj, axis=0, keepdims=True)
        xn = (((xj * lax.rsqrt(ms + EPS)) * qg_ref[...]) * (DIFF_HD ** -0.5)).astype(BF16)
        own, other = (q1_ref, q2_ref) if j % 2 == 0 else (q2_ref, q1_ref)
        for blk in range(nblk):
            own[blk, rs, :] = xn[:, blk * TBLK:(blk + 1) * TBLK]
            other[blk, rs, :] = zeros
    for blk in range(nblk):
        vt_ref[blk] = v_t[:, blk * TBLK:(blk + 1) * TBLK].astype(BF16)
    ks = _half_rms(jnp.dot(hb, w_ref[:, 0:512], preferred_element_type=F32), kg_ref[...])
    for h, kh in enumerate(ks):
        k_ref[:, h * LANES:(h + 1) * LANES] = kh.astype(BF16)
    z_ref[...] = jnp.dot(hb, w_ref[:, 512:1024], preferred_element_type=F32)


def _row_spec(width, rows=PROJ_ROWS):
    return pl.BlockSpec((rows, width), lambda i: (i, 0))


def _full_spec(shape):
    nd = len(shape)
    return pl.BlockSpec(shape, lambda i: (0,) * nd)


def _row_out(n, width, dtype):
    return jax.ShapeDtypeStruct((n, width), dtype), _row_spec(width)


def _fm_out(n, feat, dtype):
    return (jax.ShapeDtypeStruct((n // TBLK, feat, TBLK), dtype),
            pl.BlockSpec((PROJ_ROWS // TBLK, feat, TBLK), lambda i: (i, 0, 0)))


def _proj_call(kernel, name, x2, g, consts, outs):
    n = x2.shape[0]
    in_specs = [_row_spec(D_MODEL), _full_spec(g.shape)] + [_full_spec(e.shape) for e in consts]
    return pl.pallas_call(
        kernel,
        grid=(n // PROJ_ROWS,),
        in_specs=in_specs,
        out_specs=[spec for _, spec in outs],
        out_shape=[shape for shape, _ in outs],
        compiler_params=_cparams(("parallel",)),
        name=name,
    )(x2, g, *consts)


def _gla_kernel(q_ref, k_ref, v_ref, a_ref, z_ref, wa_ref, ba_ref, ng_ref, o_ref, st_ref):
    rows = q_ref.shape[0]
    nchunk = rows // CHUNK

    @pl.when(pl.program_id(1) == 0)
    def _():
        st_ref[...] = jnp.zeros_like(st_ref)

    pre = jnp.dot(a_ref[...], wa_ref[...], preferred_element_type=F32) + ba_ref[...]
    log_a = -(jnp.maximum(-pre, 0.0) + jnp.log1p(jnp.exp(-jnp.abs(pre)))) / GLA_TAU
    pos = lax.broadcasted_iota(I32, (rows, 1), 0) & (CHUNK - 1)
    cum = log_a
    s = 1
    while s < CHUNK:
        cum = cum + jnp.where(pos >= s, pltpu.roll(cum, s, axis=0), 0.0)
        s *= 2
    lo = lax.broadcasted_iota(I32, (CHUNK, LANES), 1) < GLA_DK

    for c in range(nchunk):
        r0 = c * CHUNK
        total = cum[r0 + CHUNK - 1:r0 + CHUNK, :]
        decay_to_end = jnp.exp(total - cum[r0:r0 + CHUNK, :])
        k_dec = (k_ref[r0:r0 + CHUNK, :].astype(F32) * decay_to_end).astype(BF16)
        a_tot = jnp.exp(total)
        q_c = q_ref[r0:r0 + CHUNK, :].astype(F32) * (GLA_DK ** -0.5)
        for h in range(GLA_HEADS):
            j = h // 2
            pair = slice(j * LANES, (j + 1) * LANES)
            v_h = v_ref[r0:r0 + CHUNK, h * GLA_DV:(h + 1) * GLA_DV]
            u_t = lax.dot_general(v_h, k_dec[:, pair], (((0,), (0,)), ((), ())),
                                  preferred_element_type=F32)
            st = st_ref[h] * a_tot[:, pair] + u_t
            st_ref[h] = st
            keep = lo if h % 2 == 0 else jnp.logical_not(lo)
            q_h = jnp.where(keep, q_c[:, pair], 0.0).astype(BF16)
            o = _nt(q_h, st.astype(BF16))
            ms = jnp.mean(o * o, axis=-1, keepdims=True)
            o = (o * lax.rsqrt(ms + EPS)) * ng_ref[...]
            zh = z_ref[r0:r0 + CHUNK, h * GLA_DV:(h + 1) * GLA_DV]
            o_ref[r0:r0 + CHUNK, h * GLA_DV:(h + 1) * GLA_DV] = (o * _silu(zh)).astype(BF16)


def _gla_call(q, k, v, a, z, wa, ba, ng, b, t):
    nb = t // GLA_ROWS

    def rs(width):
        return pl.BlockSpec((GLA_ROWS, width), lambda bi, i: (bi * nb + i, 0))

    def fs(shape):
        return pl.BlockSpec(shape, lambda bi, i: (0,) * len(shape))

    return pl.pallas_call(
        _gla_kernel,
        grid=(b, nb),
        in_specs=[rs(256), rs(256), rs(512), rs(128), rs(512),
                  fs(wa.shape), fs(ba.shape), fs(ng.shape)],
        out_specs=rs(512),
        out_shape=jax.ShapeDtypeStruct((b * t, BR_W), BF16),
        scratch_shapes=[pltpu.VMEM((GLA_HEADS, GLA_DV, LANES), F32)],
        compiler_params=_cparams(("parallel", "arbitrary")),
        name="gla_mixer",
    )(q, k, v, a, z, wa, ba, ng)


def _softmax_unit(s_ref, p_ref, m_ref, l_ref, i, c):
    nk = s_ref.shape[1]
    mx = None
    for r in range(0, nk, SUB):
        part = _fold(s_ref[i, r:r + SUB, :], jnp.max)
        mx = part if mx is None else jnp.maximum(mx, part)
    m_old = m_ref[i]
    m_new = jnp.maximum(m_old, jnp.max(mx, axis=0, keepdims=True))
    alpha = jnp.exp2((m_old - m_new) * c)
    ps = None
    for r in range(0, nk, SUB):
        p = jnp.exp2((s_ref[i, r:r + SUB, :] - m_new) * c)
        p_ref[i, r:r + SUB, :] = p.astype(BF16)
        part = _fold(p, jnp.sum)
        ps = part if ps is None else ps + part
    l_ref[i] = alpha * l_ref[i] + jnp.sum(ps, axis=0, keepdims=True)
    m_ref[i] = m_new
    return alpha


def _dsa_kernel(qt_ref, iqt_ref, iwt_ref, z_ref, k_ref, vt_ref, ik_ref, o_ref,
                keys_ref, iqm_ref, d_ref, acc_ref, m_ref, l_ref, s_ref, p_ref,
                *, topk, idx_bits):
    nq = qt_ref.shape[1]
    kb_sz = keys_ref.shape[1]
    nhalf = kb_sz // TBLK
    qi = pl.program_id(1)
    q0 = qi * nq
    nkb = (q0 + nq + kb_sz - 1) >> (kb_sz.bit_length() - 1)

    q_end = q0 + ((lax.broadcasted_iota(I32, (1, nq), 1) >> CHUNK_SHIFT) + 1) * CHUNK
    sub_iota = lax.broadcasted_iota(I32, (SUB, nq), 0)

    upper = lax.broadcasted_iota(I32, (LANES, nq), 0) >= IDX_HD
    for h in range(IDX_HEADS):
        pair = iqt_ref[(h // 2) * LANES:(h // 2 + 1) * LANES, :].astype(F32)
        keep = upper if h % 2 else jnp.logical_not(upper)
        iqm_ref[h] = jnp.where(keep, pair, 0.0).astype(BF16)
    wc = iwt_ref[...] * (IDX_HEADS ** -0.5 * IDX_HD ** -0.5)

    def score_body(kb, carry):
        for half in range(nhalf):
            kstart = pl.multiple_of(kb * kb_sz + half * TBLK, TBLK)
            ikb = ik_ref[pl.ds(kstart, TBLK), :]
            for h in range(IDX_HEADS):
                d_ref[h] = jnp.dot(ikb, iqm_ref[h], preferred_element_type=F32)
            for r in range(0, TBLK, SUB):
                s = jnp.zeros((SUB, nq), F32)
                for h in range(IDX_HEADS):
                    s = s + jnp.maximum(d_ref[h, r:r + SUB, :], 0.0) * wc[h:h + 1, :]
                s = jnp.where(s == 0.0, 0.0, s)
                bits = lax.bitcast_convert_type(s, I32)
                key = bits ^ ((bits >> 31) & 0x7FFFFFFF)
                pos = kstart + r + sub_iota
                keys_ref[kb, half * TBLK + r:half * TBLK + r + SUB, :] = jnp.where(
                    pos < q_end, key, INT_MIN)
        return carry

    lax.fori_loop(0, nkb, score_body, 0)


    def count(pred):
        def body(kb, part):
            for r in range(0, kb_sz, SUB):
                hit = pred(keys_ref[kb, r:r + SUB, :], kb * kb_sz + r)
                part = part + _fold(jnp.where(hit, 1.0, 0.0), jnp.sum)
            return part
        part = lax.fori_loop(0, nkb, body, jnp.zeros((SUBLANES, nq), F32))
        return jnp.sum(part, axis=0, keepdims=True)

    def bit_body(i, carry):
        res, cres = carry
        cand = res | lax.shift_left(jnp.int32(1), 31 - i)
        cand_s = cand ^ INT_MIN
        cnt = count(lambda kv, _: kv >= cand_s)
        take = cnt >= topk
        return jnp.where(take, cand, res), jnp.where(take, cnt, cres)

    res, cres = lax.fori_loop(0, 32, bit_body,
                              (jnp.zeros((1, nq), I32), jnp.zeros((1, nq), F32)))
    thr = jnp.maximum(res ^ INT_MIN, INT_MIN + 1)

    excess = jnp.where(cres > topk, 1.0, 0.0)

    @pl.when(jnp.max(excess) > 0.0)
    def _():
        need = topk - count(lambda kv, _: kv > thr)

        def idx_body(i, p):
            cand = p | lax.shift_left(jnp.int32(1), idx_bits - 1 - i)
            below = count(lambda kv, r0: jnp.logical_and(kv == thr, r0 + sub_iota < cand))
            return jnp.where(below <= need - 1.0, cand, p)

        last = lax.fori_loop(0, idx_bits, idx_body, jnp.zeros((1, nq), I32))
        excess_b = excess > 0.0

        def demote_body(kb, carry):
            for r in range(0, kb_sz, SUB):
                kv = keys_ref[kb, r:r + SUB, :]
                drop = jnp.logical_and(jnp.logical_and(kv == thr, excess_b),
                                       kb * kb_sz + r + sub_iota > last)
                keys_ref[kb, r:r + SUB, :] = jnp.where(drop, INT_MIN, kv)
            return carry

        lax.fori_loop(0, nkb, demote_body, 0)

    acc_ref[...] = jnp.zeros_like(acc_ref)
    m_ref[...] = jnp.full_like(m_ref, NEG)
    l_ref[...] = jnp.zeros_like(l_ref)
    c_exp = DSA_HD ** -0.5 * LOG2E

    def att_body(kb, carry):
        for half in range(nhalf):
            kstart = pl.multiple_of(kb * kb_sz + half * TBLK, TBLK)
            kblk = kb * nhalf + half
            for h in range(DSA_HEADS):
                hs = slice(h * DSA_HD, (h + 1) * DSA_HD)
                s_t = jnp.dot(k_ref[pl.ds(kstart, TBLK), hs], qt_ref[hs, :],
                              preferred_element_type=F32)
                sel = keys_ref[kb, half * TBLK:(half + 1) * TBLK, :] >= thr
                s_ref[h] = jnp.where(sel, s_t, NEG)
            for h in range(DSA_HEADS):
                hs = slice(h * DSA_HD, (h + 1) * DSA_HD)
                alpha = _softmax_unit(s_ref, p_ref, m_ref, l_ref, h, c_exp)
                acc_ref[h] = alpha * acc_ref[h] + jnp.dot(vt_ref[kblk, hs, :], p_ref[h],
                                                          preferred_element_type=F32)
        return carry

    lax.fori_loop(0, nkb, att_body, 0)

    for h in range(DSA_HEADS):
        hs = slice(h * DSA_HD, (h + 1) * DSA_HD)
        o = (acc_ref[h] / l_ref[h]).T
        o_ref[:, hs] = (o * _silu(z_ref[:, hs])).astype(BF16)


def _dsa_call(qt, iqt, iwt, z, k, vt, ik, b, t):
    nq = t // TBLK
    topk = min(TOPK_MAX, t // 4)
    idx_bits = max(1, (t - 1).bit_length())

    def fm_q(feat):
        return pl.BlockSpec((None, feat, TBLK), lambda bi, i: (bi * nq + i, 0, 0))

    def rows_q(width):
        return pl.BlockSpec((TBLK, width), lambda bi, i: (bi * nq + i, 0))

    def rows_b(width):
        return pl.BlockSpec((t, width), lambda bi, i: (bi, 0))

    kernel = functools.partial(_dsa_kernel, topk=topk, idx_bits=idx_bits)
    return pl.pallas_call(
        kernel,
        grid=(b, nq),
        in_specs=[fm_q(512), fm_q(512), fm_q(IDX_HEADS), rows_q(512), rows_b(512),
                  pl.BlockSpec((nq, 512, TBLK), lambda bi, i: (bi, 0, 0)), rows_b(LANES)],
        out_specs=rows_q(512),
        out_shape=jax.ShapeDtypeStruct((b * t, BR_W), BF16),
        scratch_shapes=[
            pltpu.VMEM((t // DSA_KB, DSA_KB, TBLK), I32),
            pltpu.VMEM((IDX_HEADS, LANES, TBLK), BF16),
            pltpu.VMEM((IDX_HEADS, TBLK, TBLK), F32),
            pltpu.VMEM((DSA_HEADS, DSA_HD, TBLK), F32),
            pltpu.VMEM((DSA_HEADS, 1, TBLK), F32),
            pltpu.VMEM((DSA_HEADS, 1, TBLK), F32),
            pltpu.VMEM((DSA_HEADS, TBLK, TBLK), F32),
            pltpu.VMEM((DSA_HEADS, TBLK, TBLK), BF16),
        ],
        compiler_params=_cparams(("parallel", "arbitrary")),
        name="dsa_mixer",
    )(qt, iqt, iwt, z, k, vt, ik)


def _diff_kernel(q1_ref, q2_ref, z_ref, lq1_ref, lk1_ref, lq2_ref, lk2_ref, k_ref, vt_ref,
                 o_ref, acc_ref, m_ref, l_ref, s_ref, p_ref, *, lambda_init):
    nq = q1_ref.shape[1]
    qi = pl.program_id(1)
    lam = (jnp.exp(jnp.sum(lq1_ref[...] * lk1_ref[...], axis=-1, keepdims=True))
           - jnp.exp(jnp.sum(lq2_ref[...] * lk2_ref[...], axis=-1, keepdims=True))
           + lambda_init)

    acc_ref[...] = jnp.zeros_like(acc_ref)
    m_ref[...] = jnp.full_like(m_ref, NEG)
    l_ref[...] = jnp.zeros_like(l_ref)

    q_end = ((lax.broadcasted_iota(I32, (1, nq), 1) >> CHUNK_SHIFT) + 1) * CHUNK
    allowed = lax.broadcasted_iota(I32, (nq, nq), 0) < q_end

    def block(kb, masked):
        rows = pl.ds(pl.multiple_of(kb * nq, nq), nq)
        for h in range(DIFF_HEADS):
            hs = slice(h * LANES, (h + 1) * LANES)
            k_h = k_ref[rows, hs]
            for c, q_ref in enumerate((q1_ref, q2_ref)):
                s_t = jnp.dot(k_h, q_ref[hs, :], preferred_element_type=F32)
                if masked:
                    s_t = jnp.where(allowed, s_t, NEG)
                s_ref[2 * h + c] = s_t
        for i in range(2 * DIFF_HEADS):
            hs = slice((i // 2) * LANES, (i // 2 + 1) * LANES)
            alpha = _softmax_unit(s_ref, p_ref, m_ref, l_ref, i, LOG2E)
            acc_ref[i] = alpha * acc_ref[i] + jnp.dot(vt_ref[kb, hs, :], p_ref[i],
                                                      preferred_element_type=F32)

    def body(kb, carry):
        block(kb, False)
        return carry

    lax.fori_loop(0, qi, body, 0)
    block(qi, True)

    for h in range(DIFF_HEADS):
        hs = slice(h * LANES, (h + 1) * LANES)
        o_t = acc_ref[2 * h] / l_ref[2 * h] - lam * (acc_ref[2 * h + 1] / l_ref[2 * h + 1])
        ms = jnp.mean(o_t * o_t, axis=0, keepdims=True)
        o = ((o_t * lax.rsqrt(ms + EPS)) * (1.0 - lambda_init)).T
        o_ref[:, hs] = (o * _silu(z_ref[:, hs])).astype(BF16)


def _diff_call(q1, q2, z, lq1, lk1, lq2, lk2, k, vt, b, t, lambda_init):
    nq = t // TBLK

    def fm_q():
        return pl.BlockSpec((None, 512, TBLK), lambda bi, i: (bi * nq + i, 0, 0))

    def fs(shape):
        return pl.BlockSpec(shape, lambda bi, i: (0,) * len(shape))

    kernel = functools.partial(_diff_kernel, lambda_init=lambda_init)
    return pl.pallas_call(
        kernel,
        grid=(b, nq),
        in_specs=[fm_q(), fm_q(),
                  pl.BlockSpec((TBLK, 512), lambda bi, i: (bi * nq + i, 0)),
                  fs(lq1.shape), fs(lk1.shape), fs(lq2.shape), fs(lk2.shape),
                  pl.BlockSpec((t, 512), lambda bi, i: (bi, 0)),
                  pl.BlockSpec((nq, 512, TBLK), lambda bi, i: (bi, 0, 0))],
        out_specs=pl.BlockSpec((TBLK, 512), lambda bi, i: (bi * nq + i, 0)),
        out_shape=jax.ShapeDtypeStruct((b * t, BR_W), BF16),
        scratch_shapes=[
            pltpu.VMEM((2 * DIFF_HEADS, 2 * DIFF_HD, TBLK), F32),
            pltpu.VMEM((2 * DIFF_HEADS, 1, TBLK), F32),
            pltpu.VMEM((2 * DIFF_HEADS, 1, TBLK), F32),
            pltpu.VMEM((2 * DIFF_HEADS, TBLK, TBLK), F32),
            pltpu.VMEM((2 * DIFF_HEADS, TBLK, TBLK), BF16),
        ],
        compiler_params=_cparams(("parallel", "arbitrary")),
        name="diff_mixer",
    )(q1, q2, z, lq1, lk1, lq2, lk2, k, vt)


def _out_kernel(x_ref, g_ref, wg_ref, ya_ref, yb_ref, yc_ref, wbr_ref, wo_ref, o_ref):
    x = x_ref[...]
    hb = _rms_rows(x, g_ref[...]).astype(BF16)
    merged = jnp.zeros(x.shape, F32)
    for i, y_ref in enumerate((ya_ref, yb_ref, yc_ref)):
        gate = _sigmoid(jnp.dot(hb, wg_ref[:, i * D_MODEL:(i + 1) * D_MODEL],
                                preferred_element_type=F32))
        merged = merged + gate * jnp.dot(y_ref[...], wbr_ref[i], preferred_element_type=F32)
    o_ref[...] = x + jnp.dot(merged.astype(BF16), wo_ref[...], preferred_element_type=F32)


def _out_call(x2, g, wg, ya, yb, yc, wbr, wo):
    n = x2.shape[0]
    return pl.pallas_call(
        _out_kernel,
        grid=(n // PROJ_ROWS,),
        in_specs=[_row_spec(D_MODEL), _full_spec(g.shape), _full_spec(wg.shape),
                  _row_spec(BR_W), _row_spec(BR_W), _row_spec(BR_W),
                  _full_spec(wbr.shape), _full_spec(wo.shape)],
        out_specs=_row_spec(D_MODEL),
        out_shape=jax.ShapeDtypeStruct((n, D_MODEL), F32),
        compiler_params=_cparams(("parallel",)),
        name="merge_out",
    )(x2, g, wg, ya, yb, yc, wbr, wo)


def _pad_cols(w, width):
    return jnp.pad(w, ((0, 0), (0, width - w.shape[1])))


def kernel(x, norm_g, w_in, gla_wa2, gla_ba, gla_norm_g, dsa_qn_g, dsa_kn_g, diff_qn_g,
           diff_kn_g, diff_lq1, diff_lk1, diff_lq2, diff_lk2, w_br, w_out):
    b, t, d = x.shape
    depth = w_in.shape[0]
    assert d == D_MODEL and t % max(GLA_ROWS, DSA_KB, TBLK) == 0 and (b * t) % PROJ_ROWS == 0
    n = b * t
    x2 = x.reshape(n, d)

    for l in range(depth):
        w = w_in[l]
        seg = [w[:, OFFS[i]:OFFS[i + 1]] for i in range(len(SIZES))]
        (gq, gk, gv, ga, gz, bq, bk, bv, iq, ik, iw, bz, cq, ck, cv, cz, gate) = seg
        w_gla = jnp.concatenate([gq, gk, gv, _pad_cols(ga, LANES), gz], axis=1).astype(BF16)
        wt_dsa = jnp.concatenate([bq, bv, iq, _pad_cols(iw, 16)], axis=1).T.astype(BF16)
        w_dsa = jnp.concatenate([bk, ik, ik, bz], axis=1).astype(BF16)
        wt_diff = jnp.concatenate([cq, cv], axis=1).T.astype(BF16)
        w_diff = jnp.concatenate([ck, cz], axis=1).astype(BF16)
        w_gate = gate.astype(BF16)
        g = norm_g[l].reshape(1, d)

        g_q, g_k, g_v, g_a, g_z = _proj_call(
            _proj_gla_kernel, "proj_gla", x2, g, [w_gla],
            [_row_out(n, 256, BF16), _row_out(n, 256, BF16), _row_out(n, 512, BF16),
             _row_out(n, 128, BF16), _row_out(n, 512, F32)])
        b_qt, b_k, b_vt, i_qt, i_k, i_wt, b_z = _proj_call(
            _proj_dsa_kernel, "proj_dsa", x2, g,
            [wt_dsa, w_dsa, dsa_qn_g[l].reshape(DSA_HD, 1), dsa_kn_g[l].reshape(1, DSA_HD)],
            [_fm_out(n, 512, BF16), _row_out(n, 512, BF16), _fm_out(n, 512, BF16),
             _fm_out(n, 512, BF16), _row_out(n, LANES, BF16), _fm_out(n, IDX_HEADS, F32),
             _row_out(n, 512, F32)])
        c_q1, c_q2, c_k, c_vt, c_z = _proj_call(
            _proj_diff_kernel, "proj_diff", x2, g,
            [wt_diff, w_diff, diff_qn_g[l].reshape(DIFF_HD, 1),
             jnp.tile(diff_kn_g[l], 2).reshape(1, LANES)],
            [_fm_out(n, 512, BF16), _fm_out(n, 512, BF16), _row_out(n, 512, BF16),
             _fm_out(n, 512, BF16), _row_out(n, 512, F32)])

        wa = jnp.pad(gla_wa2[l], ((0, LANES - GLA_RANK), (0, 0))).astype(BF16)
        y_a = _gla_call(g_q, g_k, g_v, g_a, g_z, wa, gla_ba[l].reshape(1, -1),
                        gla_norm_g[l].reshape(1, GLA_DV), b, t)
        y_b = _dsa_call(b_qt, i_qt, i_wt, b_z, b_k, b_vt, i_k, b, t)
        lambda_init = 0.8 - 0.6 * math.exp(-0.3 * l)
        y_c = _diff_call(c_q1, c_q2, c_z, diff_lq1[l].reshape(1, -1), diff_lk1[l].reshape(1, -1),
                         diff_lq2[l].reshape(1, -1), diff_lk2[l].reshape(1, -1), c_k, c_vt,
                         b, t, lambda_init)
        x2 = _out_call(x2, g, w_gate, y_a, y_b, y_c, w_br[l].astype(BF16),
                       w_out[l].astype(BF16))
    return x2.reshape(b, t, d)
```

```python
import functools
import math

import jax
import jax.numpy as jnp
from jax import lax
from jax.experimental import pallas as pl
from jax.experimental.pallas import tpu as pltpu

F32 = jnp.float32
BF16 = jnp.bfloat16
I32 = jnp.int32
I16 = jnp.int16

D_MODEL = 1024
CHUNK = 64
CHUNK_SHIFT = 6
EPS = 1e-6
LANES = 128
SUBLANES = 8

GLA_HEADS, GLA_DK, GLA_DV, GLA_RANK, GLA_TAU = 4, 64, 128, 16, 16.0
DSA_HEADS, DSA_HD, IDX_HEADS, IDX_HD, TOPK_MAX = 4, 128, 8, 64, 256
DIFF_HEADS, DIFF_HD = 4, 64
BR_W, N_BRANCH = 512, 3

SIZES = [
    GLA_HEADS * GLA_DK, GLA_HEADS * GLA_DK, GLA_HEADS * GLA_DV, GLA_RANK, BR_W,
    DSA_HEADS * DSA_HD, DSA_HEADS * DSA_HD, DSA_HEADS * DSA_HD,
    IDX_HEADS * IDX_HD, IDX_HD, IDX_HEADS, BR_W,
    DIFF_HEADS * 2 * DIFF_HD, DIFF_HEADS * 2 * DIFF_HD, DIFF_HEADS * 2 * DIFF_HD, BR_W,
    N_BRANCH * D_MODEL,
]
OFFS = [0]
for _s in SIZES:
    OFFS.append(OFFS[-1] + _s)

INT_MIN = -(2 ** 31)
NEG = -1e30
LOG2E = math.log2(math.e)

PROJ_ROWS = 512
GLA_ROWS = 512
TBLK = 256
DSA_KB = 512
SUB = 64
PACK = 16
HEAD_W = 128
VROWS = HEAD_W + PACK
I16_MIN = -(2 ** 15)
VMEM_LIMIT = 56 * 1024 * 1024


def _cparams(sem):
    return pltpu.CompilerParams(dimension_semantics=sem, vmem_limit_bytes=VMEM_LIMIT)


def _resident(shape, index_map):
    return pl.BlockSpec(shape, index_map, pipeline_mode=pl.Buffered(1))


def _sigmoid(x):
    return 1.0 / (1.0 + jnp.exp(-x))


def _silu(x):
    return x * _sigmoid(x)


def _rms_rows(x, g):
    ms = jnp.mean(x * x, axis=-1, keepdims=True)
    return (x * lax.rsqrt(ms + EPS)) * g


def _nt(a, b):
    return lax.dot_general(a, b, (((1,), (1,)), ((), ())), preferred_element_type=F32)


def _fold(x, op):
    return op(x.reshape(x.shape[0] // SUBLANES, SUBLANES, x.shape[1]), axis=0)


def _proj_gla_kernel(x_ref, g_ref, w_ref, q_ref, k_ref, v_ref, a_ref, z_ref):
    hb = _rms_rows(x_ref[...], g_ref[...]).astype(BF16)

    def seg(a, b):
        return jnp.dot(hb, w_ref[:, a:b], preferred_element_type=F32)

    q_ref[...] = seg(0, 256).astype(BF16)
    k_ref[...] = seg(256, 512).astype(BF16)
    v_ref[...] = seg(512, 1024).astype(BF16)
    a_ref[...] = seg(1024, 1152).astype(BF16)
    z_ref[...] = seg(1152, 1664)


def _head_rms(y, g):
    outs = []
    for h in range(4):
        yh = y[:, h * LANES:(h + 1) * LANES]
        ms = jnp.mean(yh * yh, axis=-1, keepdims=True)
        outs.append((yh * lax.rsqrt(ms + EPS)) * g)
    return outs


def _store_values(vt_ref, blk, v_t):
    ones = jnp.ones((PACK, v_t.shape[1]), BF16)
    for h in range(4):
        vt_ref[blk, h * VROWS:h * VROWS + HEAD_W, :] = v_t[h * HEAD_W:(h + 1) * HEAD_W, :]
        vt_ref[blk, h * VROWS + HEAD_W:(h + 1) * VROWS, :] = ones


def _proj_dsa_kernel(x_ref, g_ref, wt_ref, w_ref, qg_ref, kg_ref,
                     qt_ref, k_ref, vt_ref, iqt_ref, ik_ref, iwt_ref, z_ref):
    hb = _rms_rows(x_ref[...], g_ref[...]).astype(BF16)
    nblk = hb.shape[0] // TBLK

    def seg(a, b):
        return jnp.dot(hb, w_ref[:, a:b], preferred_element_type=F32)

    def seg_t(a, b):
        return _nt(wt_ref[a:b, :], hb)

    q_t = seg_t(0, 512)
    for h in range(DSA_HEADS):
        rs = slice(h * DSA_HD, (h + 1) * DSA_HD)
        xh = q_t[rs, :]
        ms = jnp.mean(xh * xh, axis=0, keepdims=True)
        xn = (((xh * lax.rsqrt(ms + EPS)) * qg_ref[...]) * (DSA_HD ** -0.5 * LOG2E)).astype(BF16)
        for blk in range(nblk):
            qt_ref[blk, rs, :] = xn[:, blk * TBLK:(blk + 1) * TBLK]
    v_t = seg_t(512, 1024).astype(BF16)
    iq_t = seg_t(1024, 1536).astype(BF16)
    iw_t = seg_t(1536, 1552)
    for blk in range(nblk):
        cs = slice(blk * TBLK, (blk + 1) * TBLK)
        _store_values(vt_ref, blk, v_t[:, cs])
        iqt_ref[blk] = iq_t[:, cs]
        iwt_ref[blk] = iw_t[0:IDX_HEADS, cs]
    for h, kh in enumerate(_head_rms(seg(0, 512), kg_ref[...])):
        k_ref[:, h * LANES:(h + 1) * LANES] = kh.astype(BF16)
    ik_ref[...] = seg(512, 640).astype(BF16)
    z_ref[...] = seg(640, 1152)


def _half_rms(y, g2):
    lo = lax.broadcasted_iota(I32, (y.shape[0], LANES), 1) < DIFF_HD
    outs = []
    for h in range(4):
        yh = y[:, h * LANES:(h + 1) * LANES]
        sq = yh * yh
        s_lo = jnp.sum(jnp.where(lo, sq, 0.0), axis=-1, keepdims=True)
        s_hi = jnp.sum(jnp.where(lo, 0.0, sq), axis=-1, keepdims=True)
        ms = jnp.where(lo, s_lo, s_hi) * (1.0 / DIFF_HD)
        outs.append((yh * lax.rsqrt(ms + EPS)) * g2)
    return outs


def _proj_diff_kernel(x_ref, g_ref, wt_ref, w_ref, qg_ref, kg_ref,
                      q1_ref, q2_ref, k_ref, vt_ref, z_ref):
    hb = _rms_rows(x_ref[...], g_ref[...]).astype(BF16)
    nblk = hb.shape[0] // TBLK
    q_t = _nt(wt_ref[0:512, :], hb)
    v_t = _nt(wt_ref[512:1024, :], hb)
    zeros = jnp.zeros((DIFF_HD, TBLK), BF16)
    for j in range(2 * DIFF_HEADS):
        rs = slice(j * DIFF_HD, (j + 1) * DIFF_HD)
        xj = q_t[rs, :]
        ms = jnp.mean(xj * xj, axis=0, keepdims=True)
        xn = (((xj * lax.rsqrt(ms + EPS)) * qg_ref[...]) * (DIFF_HD ** -0.5 * LOG2E)).astype(BF16)
        own, other = (q1_ref, q2_ref) if j % 2 == 0 else (q2_ref, q1_ref)
        for blk in range(nblk):
            own[blk, rs, :] = xn[:, blk * TBLK:(blk + 1) * TBLK]
            other[blk, rs, :] = zeros
    for blk in range(nblk):
        _store_values(vt_ref, blk, v_t[:, blk * TBLK:(blk + 1) * TBLK].astype(BF16))
    ks = _half_rms(jnp.dot(hb, w_ref[:, 0:512], preferred_element_type=F32), kg_ref[...])
    for h, kh in enumerate(ks):
        k_ref[:, h * LANES:(h + 1) * LANES] = kh.astype(BF16)
    z_ref[...] = jnp.dot(hb, w_ref[:, 512:1024], preferred_element_type=F32)


def _row_spec(width, rows=PROJ_ROWS):
    return pl.BlockSpec((rows, width), lambda i: (i, 0))


def _full_spec(shape):
    nd = len(shape)
    return pl.BlockSpec(shape, lambda i: (0,) * nd)


def _row_out(n, width, dtype):
    return jax.ShapeDtypeStruct((n, width), dtype), _row_spec(width)


def _fm_out(n, feat, dtype):
    return (jax.ShapeDtypeStruct((n // TBLK, feat, TBLK), dtype),
            pl.BlockSpec((PROJ_ROWS // TBLK, feat, TBLK), lambda i: (i, 0, 0)))


def _proj_call(kernel, name, x2, g, consts, outs):
    n = x2.shape[0]
    in_specs = [_row_spec(D_MODEL), _full_spec(g.shape)] + [_full_spec(e.shape) for e in consts]
    return pl.pallas_call(
        kernel,
        grid=(n // PROJ_ROWS,),
        in_specs=in_specs,
        out_specs=[spec for _, spec in outs],
        out_shape=[shape for shape, _ in outs],
        compiler_params=_cparams(("parallel",)),
        name=name,
    )(x2, g, *consts)


def _gla_kernel(q_ref, k_ref, v_ref, a_ref, z_ref, wa_ref, ba_ref, ng_ref, o_ref, st_ref):
    rows = q_ref.shape[0]
    nchunk = rows // CHUNK

    @pl.when(pl.program_id(1) == 0)
    def _():
        st_ref[...] = jnp.zeros_like(st_ref)

    pre = jnp.dot(a_ref[...], wa_ref[...], preferred_element_type=F32) + ba_ref[...]
    log_a = -(jnp.maximum(-pre, 0.0) + jnp.log1p(jnp.exp(-jnp.abs(pre)))) / GLA_TAU
    pos = lax.broadcasted_iota(I32, (rows, 1), 0) & (CHUNK - 1)
    cum = log_a
    s = 1
    while s < CHUNK:
        cum = cum + jnp.where(pos >= s, pltpu.roll(cum, s, axis=0), 0.0)
        s *= 2
    lo = lax.broadcasted_iota(I32, (CHUNK, LANES), 1) < GLA_DK

    for c in range(nchunk):
        r0 = c * CHUNK
        total = cum[r0 + CHUNK - 1:r0 + CHUNK, :]
        decay_to_end = jnp.exp(total - cum[r0:r0 + CHUNK, :])
        k_dec = (k_ref[r0:r0 + CHUNK, :].astype(F32) * decay_to_end).astype(BF16)
        a_tot = jnp.exp(total)
        q_c = q_ref[r0:r0 + CHUNK, :].astype(F32) * (GLA_DK ** -0.5)
        for h in range(GLA_HEADS):
            j = h // 2
            pair = slice(j * LANES, (j + 1) * LANES)
            v_h = v_ref[r0:r0 + CHUNK, h * GLA_DV:(h + 1) * GLA_DV]
            u_t = lax.dot_general(v_h, k_dec[:, pair], (((0,), (0,)), ((), ())),
                                  preferred_element_type=F32)
            st = st_ref[h] * a_tot[:, pair] + u_t
            st_ref[h] = st
            keep = lo if h % 2 == 0 else jnp.logical_not(lo)
            q_h = jnp.where(keep, q_c[:, pair], 0.0).astype(BF16)
            o = _nt(q_h, st.astype(BF16))
            ms = jnp.mean(o * o, axis=-1, keepdims=True)
            o = (o * lax.rsqrt(ms + EPS)) * ng_ref[...]
            zh = z_ref[r0:r0 + CHUNK, h * GLA_DV:(h + 1) * GLA_DV]
            o_ref[r0:r0 + CHUNK, h * GLA_DV:(h + 1) * GLA_DV] = (o * _silu(zh)).astype(BF16)


def _gla_call(q, k, v, a, z, wa, ba, ng, b, t):
    nb = t // GLA_ROWS

    def rs(width):
        return pl.BlockSpec((GLA_ROWS, width), lambda bi, i: (bi * nb + i, 0))

    def fs(shape):
        return pl.BlockSpec(shape, lambda bi, i: (0,) * len(shape))

    return pl.pallas_call(
        _gla_kernel,
        grid=(b, nb),
        in_specs=[rs(256), rs(256), rs(512), rs(128), rs(512),
                  fs(wa.shape), fs(ba.shape), fs(ng.shape)],
        out_specs=rs(512),
        out_shape=jax.ShapeDtypeStruct((b * t, BR_W), BF16),
        scratch_shapes=[pltpu.VMEM((GLA_HEADS, GLA_DV, LANES), F32)],
        compiler_params=_cparams(("parallel", "arbitrary")),
        name="gla_mixer",
    )(q, k, v, a, z, wa, ba, ng)


def _softmax_unit(s_ref, p_ref, m_ref, i, allowed=None):
    nk = s_ref.shape[1]

    def piece(r):
        s = s_ref[i, r:r + SUB, :]
        return s if allowed is None else jnp.where(allowed(r), s, NEG)

    mx = None
    for r in range(0, nk, SUB):
        part = _fold(piece(r), jnp.max)
        mx = part if mx is None else jnp.maximum(mx, part)
    m_old = m_ref[i]
    m_new = jnp.maximum(m_old, jnp.max(mx, axis=0, keepdims=True))
    for r in range(0, nk, SUB):
        p_ref[i, r:r + SUB, :] = jnp.exp2((piece(r) - m_new).astype(BF16))
    m_ref[i] = m_new
    return jnp.exp2(m_old - m_new)


def _dsa_kernel(qt_ref, iqt_ref, iwt_ref, z_ref, k_ref, vt_ref, ik_ref, o_ref,
                keys_ref, hi_ref, lo_ref, iqm_ref, d_ref, acc_ref, m_ref, s0_ref, s1_ref, p_ref,
                *, topk, idx_bits):
    nq = qt_ref.shape[1]
    kb_sz = keys_ref.shape[1]
    nhalf = kb_sz // TBLK
    qi = pl.program_id(1)
    q0 = qi * nq
    nkb = (q0 + nq + kb_sz - 1) >> (kb_sz.bit_length() - 1)

    q_end = q0 + ((lax.broadcasted_iota(I32, (1, nq), 1) >> CHUNK_SHIFT) + 1) * CHUNK
    sub_iota = lax.broadcasted_iota(I32, (SUB, nq), 0)

    upper = lax.broadcasted_iota(I32, (LANES, nq), 0) >= IDX_HD
    for h in range(IDX_HEADS):
        pair = iqt_ref[(h // 2) * LANES:(h // 2 + 1) * LANES, :].astype(F32)
        keep = upper if h % 2 else jnp.logical_not(upper)
        iqm_ref[h] = jnp.where(keep, pair, 0.0).astype(BF16)
    wc = iwt_ref[...] * (IDX_HEADS ** -0.5 * IDX_HD ** -0.5)

    def score_body(kb, carry):
        for half in range(nhalf):
            kstart = pl.multiple_of(kb * kb_sz + half * TBLK, TBLK)
            ikb = ik_ref[pl.ds(kstart, TBLK), :]
            for h in range(IDX_HEADS):
                d_ref[h] = jnp.dot(ikb, iqm_ref[h], preferred_element_type=F32)
            for r in range(0, TBLK, SUB):
                s = jnp.zeros((SUB, nq), F32)
                for h in range(IDX_HEADS):
                    s = s + jnp.maximum(d_ref[h, r:r + SUB, :], 0.0) * wc[h:h + 1, :]
                s = jnp.where(s == 0.0, 0.0, s)
                bits = lax.bitcast_convert_type(s, I32)
                key = bits ^ ((bits >> 31) & 0x7FFFFFFF)
                pos = kstart + r + sub_iota
                key = jnp.where(pos < q_end, key, INT_MIN)
                rs = slice(half * TBLK + r, half * TBLK + r + SUB)
                keys_ref[kb, rs, :] = key
                hi_ref[kb, rs, :] = (key >> 16).astype(I16)
                lo_ref[kb, rs, :] = ((key & 0xFFFF) + I16_MIN).astype(I16)
        return carry

    lax.fori_loop(0, nkb, score_body, 0)


    def count(pred):
        def body(kb, part):
            for r in range(0, kb_sz, SUB):
                hit = pred(keys_ref[kb, r:r + SUB, :], kb * kb_sz + r)
                part = part + _fold(jnp.where(hit, 1.0, 0.0), jnp.sum)
            return part
        part = lax.fori_loop(0, nkb, body, jnp.zeros((SUBLANES, nq), F32))
        return jnp.sum(part, axis=0, keepdims=True)

    n_acc = 4
    piece = n_acc * PACK * 2
    one16, zero16 = jnp.ones((), I16), jnp.zeros((), I16)

    def count16(plane_ref, pred):
        def body(kb, accs):
            accs = list(accs)
            for r in range(0, kb_sz, piece):
                ones = jnp.where(pred(plane_ref[kb, r:r + piece, :]), one16, zero16)
                for j in range(piece // PACK):
                    accs[j % n_acc] = accs[j % n_acc] + ones[j * PACK:(j + 1) * PACK, :]
            return tuple(accs)
        accs = lax.fori_loop(0, nkb, body, (jnp.zeros((PACK, nq), I16),) * n_acc)
        tot = (accs[0] + accs[1]) + (accs[2] + accs[3])
        return jnp.sum(tot.astype(I32).astype(F32), axis=0, keepdims=True)

    def search16(plane_ref, want, c_start):
        def bit_body(i, carry):
            res, cres = carry
            cand = res | lax.shift_left(jnp.int32(1), 15 - i)
            cand16 = (cand + I16_MIN).astype(I16)
            cnt = count16(plane_ref, lambda kv: kv >= cand16)
            take = cnt >= want
            return jnp.where(take, cand, res), jnp.where(take, cnt, cres)
        return lax.fori_loop(0, 16, bit_body, (jnp.zeros((1, nq), I32), c_start))

    zero_f = jnp.zeros((1, nq), F32)
    hi_u, c_ge_hi = search16(hi_ref, jnp.full((1, nq), float(topk), F32), zero_f)
    hi16 = (hi_u + I16_MIN).astype(I16)
    c_gt_hi = count16(hi_ref, lambda kv: kv > hi16)
    low_min = jnp.full((), I16_MIN, I16)

    def bucket_body(kb, carry):
        for r in range(0, kb_sz, piece):
            rs = slice(r, r + piece)
            lo_ref[kb, rs, :] = jnp.where(hi_ref[kb, rs, :] == hi16, lo_ref[kb, rs, :], low_min)
        return carry

    lax.fori_loop(0, nkb, bucket_body, 0)
    lo_u, c_ge_lo = search16(lo_ref, topk - c_gt_hi, c_ge_hi - c_gt_hi)
    thr = ((hi_u + I16_MIN) << 16) | lo_u
    n_ge = c_gt_hi + c_ge_lo

    excess = jnp.where(jnp.logical_and(n_ge > topk, thr > INT_MIN), 1.0, 0.0)
    thr = jnp.maximum(thr, INT_MIN + 1)

    @pl.when(jnp.max(excess) > 0.0)
    def _():
        need = topk - count(lambda kv, _: kv > thr)

        def idx_body(i, p):
            cand = p | lax.shift_left(jnp.int32(1), idx_bits - 1 - i)
            below = count(lambda kv, r0: jnp.logical_and(kv == thr, r0 + sub_iota < cand))
            return jnp.where(below <= need - 1.0, cand, p)

        last = lax.fori_loop(0, idx_bits, idx_body, jnp.zeros((1, nq), I32))
        excess_b = excess > 0.0

        def demote_body(kb, carry):
            for r in range(0, kb_sz, SUB):
                kv = keys_ref[kb, r:r + SUB, :]
                drop = jnp.logical_and(jnp.logical_and(kv == thr, excess_b),
                                       kb * kb_sz + r + sub_iota > last)
                keys_ref[kb, r:r + SUB, :] = jnp.where(drop, INT_MIN, kv)
            return carry

        lax.fori_loop(0, nkb, demote_body, 0)

    acc_ref[...] = jnp.zeros_like(acc_ref)
    m_ref[...] = jnp.full_like(m_ref, NEG)

    nblk = (q0 + nq) >> (TBLK.bit_length() - 1)
    half_shift = nhalf.bit_length() - 1

    def logits(j, s_ref):
        kstart = pl.multiple_of(j * TBLK, TBLK)
        krows = pl.ds(pl.multiple_of((j & (nhalf - 1)) * TBLK, TBLK), TBLK)
        for h in range(DSA_HEADS):
            hs = slice(h * DSA_HD, (h + 1) * DSA_HD)
            s_t = jnp.dot(k_ref[pl.ds(kstart, TBLK), hs], qt_ref[hs, :],
                          preferred_element_type=F32)
            sel = keys_ref[j >> half_shift, krows, :] >= thr
            s_ref[h] = jnp.where(sel, s_t, NEG)

    def consume(j, s_ref):
        for h in range(DSA_HEADS):
            alpha = _softmax_unit(s_ref, p_ref, m_ref, h)
            acc_ref[h] = alpha * acc_ref[h] + jnp.dot(
                vt_ref[j, h * VROWS:(h + 1) * VROWS, :], p_ref[h],
                preferred_element_type=F32)

    def pair_body(t, carry):
        logits(2 * t + 1, s1_ref)
        consume(2 * t, s0_ref)
        logits(jnp.minimum(2 * t + 2, nblk - 1), s0_ref)
        consume(2 * t + 1, s1_ref)
        return carry

    logits(jnp.int32(0), s0_ref)
    lax.fori_loop(0, nblk >> 1, pair_body, 0)

    @pl.when((nblk & 1) == 1)
    def _():
        consume(nblk - 1, s0_ref)

    for h in range(DSA_HEADS):
        hs = slice(h * DSA_HD, (h + 1) * DSA_HD)
        o = (acc_ref[h, 0:HEAD_W, :] / acc_ref[h, HEAD_W:HEAD_W + 1, :]).T
        o_ref[:, hs] = (o * _silu(z_ref[:, hs])).astype(BF16)


def _dsa_call(qt, iqt, iwt, z, k, vt, ik, b, t):
    nq = t // TBLK
    topk = min(TOPK_MAX, t // 4)
    idx_bits = max(1, (t - 1).bit_length())

    def fm_q(feat):
        return pl.BlockSpec((None, feat, TBLK), lambda bi, i: (bi * nq + i, 0, 0))

    def rows_q(width):
        return pl.BlockSpec((TBLK, width), lambda bi, i: (bi * nq + i, 0))

    def rows_b(width):
        return _resident((t, width), lambda bi, i: (bi, 0))

    kernel = functools.partial(_dsa_kernel, topk=topk, idx_bits=idx_bits)
    return pl.pallas_call(
        kernel,
        grid=(b, nq),
        in_specs=[fm_q(512), fm_q(512), fm_q(IDX_HEADS), rows_q(512), rows_b(512),
                  _resident((nq, 4 * VROWS, TBLK), lambda bi, i: (bi, 0, 0)), rows_b(LANES)],
        out_specs=rows_q(512),
        out_shape=jax.ShapeDtypeStruct((b * t, BR_W), BF16),
        scratch_shapes=[
            pltpu.VMEM((t // DSA_KB, DSA_KB, TBLK), I32),
            pltpu.VMEM((t // DSA_KB, DSA_KB, TBLK), I16),
            pltpu.VMEM((t // DSA_KB, DSA_KB, TBLK), I16),
            pltpu.VMEM((IDX_HEADS, LANES, TBLK), BF16),
            pltpu.VMEM((IDX_HEADS, TBLK, TBLK), F32),
            pltpu.VMEM((DSA_HEADS, VROWS, TBLK), F32),
            pltpu.VMEM((DSA_HEADS, 1, TBLK), F32),
            pltpu.VMEM((DSA_HEADS, TBLK, TBLK), F32),
            pltpu.VMEM((DSA_HEADS, TBLK, TBLK), F32),
            pltpu.VMEM((DSA_HEADS, TBLK, TBLK), BF16),
        ],
        compiler_params=_cparams(("parallel", "arbitrary")),
        name="dsa_mixer",
    )(qt, iqt, iwt, z, k, vt, ik)


def _diff_kernel(q1_ref, q2_ref, z_ref, lq1_ref, lk1_ref, lq2_ref, lk2_ref, k_ref, vt_ref,
                 o_ref, acc_ref, m_ref, s0_ref, s1_ref, p_ref, *, lambda_init):
    nq = q1_ref.shape[1]
    qi = pl.program_id(1)
    lam = (jnp.exp(jnp.sum(lq1_ref[...] * lk1_ref[...], axis=-1, keepdims=True))
           - jnp.exp(jnp.sum(lq2_ref[...] * lk2_ref[...], axis=-1, keepdims=True))
           + lambda_init)

    acc_ref[...] = jnp.zeros_like(acc_ref)
    m_ref[...] = jnp.full_like(m_ref, NEG)

    q_end = ((lax.broadcasted_iota(I32, (1, nq), 1) >> CHUNK_SHIFT) + 1) * CHUNK
    sub_iota = lax.broadcasted_iota(I32, (SUB, nq), 0)

    def logits(kb, s_ref):
        rows = pl.ds(pl.multiple_of(kb * nq, nq), nq)
        for h in range(DIFF_HEADS):
            hs = slice(h * LANES, (h + 1) * LANES)
            k_h = k_ref[rows, hs]
            for c, q_ref in enumerate((q1_ref, q2_ref)):
                s_ref[2 * h + c] = jnp.dot(k_h, q_ref[hs, :], preferred_element_type=F32)

    def consume(kb, s_ref, allowed=None):
        for i in range(2 * DIFF_HEADS):
            vs = slice((i // 2) * VROWS, (i // 2 + 1) * VROWS)
            alpha = _softmax_unit(s_ref, p_ref, m_ref, i, allowed)
            acc_ref[i] = alpha * acc_ref[i] + jnp.dot(vt_ref[kb, vs, :], p_ref[i],
                                                      preferred_element_type=F32)

    def pair_body(t, carry):
        logits(2 * t + 1, s1_ref)
        consume(2 * t, s0_ref)
        logits(2 * t + 2, s0_ref)
        consume(2 * t + 1, s1_ref)
        return carry

    def chunk_mask(r):
        return r + sub_iota < q_end

    logits(jnp.int32(0), s0_ref)
    lax.fori_loop(0, qi >> 1, pair_body, 0)

    @pl.when((qi & 1) == 0)
    def _():
        consume(qi, s0_ref, chunk_mask)

    @pl.when((qi & 1) == 1)
    def _():
        logits(qi, s1_ref)
        consume(qi - 1, s0_ref)
        consume(qi, s1_ref, chunk_mask)

    def normalised(i):
        return acc_ref[i, 0:HEAD_W, :] / acc_ref[i, HEAD_W:HEAD_W + 1, :]

    for h in range(DIFF_HEADS):
        hs = slice(h * LANES, (h + 1) * LANES)
        o_t = normalised(2 * h) - lam * normalised(2 * h + 1)
        ms = jnp.mean(o_t * o_t, axis=0, keepdims=True)
        o = ((o_t * lax.rsqrt(ms + EPS)) * (1.0 - lambda_init)).T
        o_ref[:, hs] = (o * _silu(z_ref[:, hs])).astype(BF16)


def _diff_call(q1, q2, z, lq1, lk1, lq2, lk2, k, vt, b, t, lambda_init):
    nq = t // TBLK

    def fm_q():
        return pl.BlockSpec((None, 512, TBLK), lambda bi, i: (bi * nq + i, 0, 0))

    def fs(shape):
        return pl.BlockSpec(shape, lambda bi, i: (0,) * len(shape))

    kernel = functools.partial(_diff_kernel, lambda_init=lambda_init)
    return pl.pallas_call(
        kernel,
        grid=(b, nq),
        in_specs=[fm_q(), fm_q(),
                  pl.BlockSpec((TBLK, 512), lambda bi, i: (bi * nq + i, 0)),
                  fs(lq1.shape), fs(lk1.shape), fs(lq2.shape), fs(lk2.shape),
                  _resident((t, 512), lambda bi, i: (bi, 0)),
                  _resident((nq, 4 * VROWS, TBLK), lambda bi, i: (bi, 0, 0))],
        out_specs=pl.BlockSpec((TBLK, 512), lambda bi, i: (bi * nq + i, 0)),
        out_shape=jax.ShapeDtypeStruct((b * t, BR_W), BF16),
        scratch_shapes=[
            pltpu.VMEM((2 * DIFF_HEADS, VROWS, TBLK), F32),
            pltpu.VMEM((2 * DIFF_HEADS, 1, TBLK), F32),
            pltpu.VMEM((2 * DIFF_HEADS, TBLK, TBLK), F32),
            pltpu.VMEM((2 * DIFF_HEADS, TBLK, TBLK), F32),
            pltpu.VMEM((2 * DIFF_HEADS, TBLK, TBLK), BF16),
        ],
        compiler_params=_cparams(("parallel", "arbitrary")),
        name="diff_mixer",
    )(q1, q2, z, lq1, lk1, lq2, lk2, k, vt)


def _out_kernel(x_ref, g_ref, wg_ref, ya_ref, yb_ref, yc_ref, wbr_ref, wo_ref, o_ref):
    x = x_ref[...]
    hb = _rms_rows(x, g_ref[...]).astype(BF16)
    merged = jnp.zeros(x.shape, F32)
    for i, y_ref in enumerate((ya_ref, yb_ref, yc_ref)):
        gate = _sigmoid(jnp.dot(hb, wg_ref[:, i * D_MODEL:(i + 1) * D_MODEL],
                                preferred_element_type=F32))
        merged = merged + gate * jnp.dot(y_ref[...], wbr_ref[i], preferred_element_type=F32)
    o_ref[...] = x + jnp.dot(merged.astype(BF16), wo_ref[...], preferred_element_type=F32)


def _out_call(x2, g, wg, ya, yb, yc, wbr, wo):
    n = x2.shape[0]
    return pl.pallas_call(
        _out_kernel,
        grid=(n // PROJ_ROWS,),
        in_specs=[_row_spec(D_MODEL), _full_spec(g.shape), _full_spec(wg.shape),
                  _row_spec(BR_W), _row_spec(BR_W), _row_spec(BR_W),
                  _full_spec(wbr.shape), _full_spec(wo.shape)],
        out_specs=_row_spec(D_MODEL),
        out_shape=jax.ShapeDtypeStruct((n, D_MODEL), F32),
        compiler_params=_cparams(("parallel",)),
        name="merge_out",
    )(x2, g, wg, ya, yb, yc, wbr, wo)


def _pad_cols(w, width):
    return jnp.pad(w, ((0, 0), (0, width - w.shape[1])))


def kernel(x, norm_g, w_in, gla_wa2, gla_ba, gla_norm_g, dsa_qn_g, dsa_kn_g, diff_qn_g,
           diff_kn_g, diff_lq1, diff_lk1, diff_lq2, diff_lk2, w_br, w_out):
    b, t, d = x.shape
    depth = w_in.shape[0]
    assert d == D_MODEL and t % max(GLA_ROWS, DSA_KB, TBLK) == 0 and (b * t) % PROJ_ROWS == 0
    n = b * t
    x2 = x.reshape(n, d)

    for l in range(depth):
        w = w_in[l]
        seg = [w[:, OFFS[i]:OFFS[i + 1]] for i in range(len(SIZES))]
        (gq, gk, gv, ga, gz, bq, bk, bv, iq, ik, iw, bz, cq, ck, cv, cz, gate) = seg
        w_gla = jnp.concatenate([gq, gk, gv, _pad_cols(ga, LANES), gz], axis=1).astype(BF16)
        wt_dsa = jnp.concatenate([bq, bv, iq, _pad_cols(iw, 16)], axis=1).T.astype(BF16)
        w_dsa = jnp.concatenate([bk, ik, ik, bz], axis=1).astype(BF16)
        wt_diff = jnp.concatenate([cq, cv], axis=1).T.astype(BF16)
        w_diff = jnp.concatenate([ck, cz], axis=1).astype(BF16)
        w_gate = gate.astype(BF16)
        g = norm_g[l].reshape(1, d)

        g_q, g_k, g_v, g_a, g_z = _proj_call(
            _proj_gla_kernel, "proj_gla", x2, g, [w_gla],
            [_row_out(n, 256, BF16), _row_out(n, 256, BF16), _row_out(n, 512, BF16),
             _row_out(n, 128, BF16), _row_out(n, 512, F32)])
        b_qt, b_k, b_vt, i_qt, i_k, i_wt, b_z = _proj_call(
            _proj_dsa_kernel, "proj_dsa", x2, g,
            [wt_dsa, w_dsa, dsa_qn_g[l].reshape(DSA_HD, 1), dsa_kn_g[l].reshape(1, DSA_HD)],
            [_fm_out(n, 512, BF16), _row_out(n, 512, BF16), _fm_out(n, 4 * VROWS, BF16),
             _fm_out(n, 512, BF16), _row_out(n, LANES, BF16), _fm_out(n, IDX_HEADS, F32),
             _row_out(n, 512, F32)])
        c_q1, c_q2, c_k, c_vt, c_z = _proj_call(
            _proj_diff_kernel, "proj_diff", x2, g,
            [wt_diff, w_diff, diff_qn_g[l].reshape(DIFF_HD, 1),
             jnp.tile(diff_kn_g[l], 2).reshape(1, LANES)],
            [_fm_out(n, 512, BF16), _fm_out(n, 512, BF16), _row_out(n, 512, BF16),
             _fm_out(n, 4 * VROWS, BF16), _row_out(n, 512, F32)])

        wa = jnp.pad(gla_wa2[l], ((0, LANES - GLA_RANK), (0, 0))).astype(BF16)
        y_a = _gla_call(g_q, g_k, g_v, g_a, g_z, wa, gla_ba[l].reshape(1, -1),
                        gla_norm_g[l].reshape(1, GLA_DV), b, t)
        y_b = _dsa_call(b_qt, i_qt, i_wt, b_z, b_k, b_vt, i_k, b, t)
        lambda_init = 0.8 - 0.6 * math.exp(-0.3 * l)
        y_c = _diff_call(c_q1, c_q2, c_z, diff_lq1[l].reshape(1, -1), diff_lk1[l].reshape(1, -1),
                         diff_lq2[l].reshape(1, -1), diff_lk2[l].reshape(1, -1), c_k, c_vt,
                         b, t, lambda_init)
        x2 = _out_call(x2, g, w_gate, y_a, y_b, y_c, w_br[l].astype(BF16),
                       w_out[l].astype(BF16))
    return x2.reshape(b, t, d)
```

```python
import functools
import math

import jax
import jax.numpy as jnp
from jax import lax
from jax.experimental import pallas as pl
from jax.experimental.pallas import tpu as pltpu

F32 = jnp.float32
BF16 = jnp.bfloat16
I32 = jnp.int32
I16 = jnp.int16

D_MODEL = 1024
CHUNK = 64
CHUNK_SHIFT = 6
EPS = 1e-6
LANES = 128
SUBLANES = 8

GLA_HEADS, GLA_DK, GLA_DV, GLA_RANK, GLA_TAU = 4, 64, 128, 16, 16.0
DSA_HEADS, DSA_HD, IDX_HEADS, IDX_HD, TOPK_MAX = 4, 128, 8, 64, 256
DIFF_HEADS, DIFF_HD = 4, 64
BR_W, N_BRANCH = 512, 3

SIZES = [
    GLA_HEADS * GLA_DK, GLA_HEADS * GLA_DK, GLA_HEADS * GLA_DV, GLA_RANK, BR_W,
    DSA_HEADS * DSA_HD, DSA_HEADS * DSA_HD, DSA_HEADS * DSA_HD,
    IDX_HEADS * IDX_HD, IDX_HD, IDX_HEADS, BR_W,
    DIFF_HEADS * 2 * DIFF_HD, DIFF_HEADS * 2 * DIFF_HD, DIFF_HEADS * 2 * DIFF_HD, BR_W,
    N_BRANCH * D_MODEL,
]
OFFS = [0]
for _s in SIZES:
    OFFS.append(OFFS[-1] + _s)

INT_MIN = -(2 ** 31)
NEG = -1e30
LOG2E = math.log2(math.e)
NORM_SLACK = 1.02
MAX_UNSHIFTED_LOG2 = 60.0

PROJ_ROWS = 512
GLA_ROWS = 512
TBLK = 256
DSA_KB = 512
SUB = 64
PACK = 16
HEAD_W = 128
VROWS = HEAD_W + PACK
I16_MIN = -(2 ** 15)
VMEM_LIMIT = 56 * 1024 * 1024


def _cparams(sem):
    return pltpu.CompilerParams(dimension_semantics=sem, vmem_limit_bytes=VMEM_LIMIT)


def _resident(shape, index_map):
    return pl.BlockSpec(shape, index_map, pipeline_mode=pl.Buffered(1))


def _sigmoid(x):
    return 1.0 / (1.0 + jnp.exp(-x))


def _silu(x):
    return x * _sigmoid(x)


def _rms_rows(x, g):
    ms = jnp.mean(x * x, axis=-1, keepdims=True)
    return (x * lax.rsqrt(ms + EPS)) * g


def _nt(a, b):
    return lax.dot_general(a, b, (((1,), (1,)), ((), ())), preferred_element_type=F32)


def _fold(x, op):
    return op(x.reshape(x.shape[0] // SUBLANES, SUBLANES, x.shape[1]), axis=0)


def _proj_gla_kernel(x_ref, g_ref, w_ref, q_ref, k_ref, v_ref, a_ref, z_ref):
    hb = _rms_rows(x_ref[...], g_ref[...]).astype(BF16)

    def seg(a, b):
        return jnp.dot(hb, w_ref[:, a:b], preferred_element_type=F32)

    q_ref[...] = seg(0, 256).astype(BF16)
    k_ref[...] = seg(256, 512).astype(BF16)
    v_ref[...] = seg(512, 1024).astype(BF16)
    a_ref[...] = seg(1024, 1152).astype(BF16)
    z_ref[...] = seg(1152, 1664)


def _head_rms(y, g):
    outs = []
    for h in range(4):
        yh = y[:, h * LANES:(h + 1) * LANES]
        ms = jnp.mean(yh * yh, axis=-1, keepdims=True)
        outs.append((yh * lax.rsqrt(ms + EPS)) * g)
    return outs


def _store_values(vt_ref, blk, v_t):
    ones = jnp.ones((PACK, v_t.shape[1]), BF16)
    for h in range(4):
        vt_ref[blk, h * VROWS:h * VROWS + HEAD_W, :] = v_t[h * HEAD_W:(h + 1) * HEAD_W, :]
        vt_ref[blk, h * VROWS + HEAD_W:(h + 1) * VROWS, :] = ones


def _proj_dsa_kernel(x_ref, g_ref, wt_ref, w_ref, qg_ref, kg_ref,
                     qt_ref, k_ref, vt_ref, iqt_ref, ik_ref, iwt_ref, z_ref):
    hb = _rms_rows(x_ref[...], g_ref[...]).astype(BF16)
    nblk = hb.shape[0] // TBLK

    def seg(a, b):
        return jnp.dot(hb, w_ref[:, a:b], preferred_element_type=F32)

    def seg_t(a, b):
        return _nt(wt_ref[a:b, :], hb)

    q_t = seg_t(0, 512)
    for h in range(DSA_HEADS):
        rs = slice(h * DSA_HD, (h + 1) * DSA_HD)
        xh = q_t[rs, :]
        ms = jnp.mean(xh * xh, axis=0, keepdims=True)
        xn = (((xh * lax.rsqrt(ms + EPS)) * qg_ref[...]) * (DSA_HD ** -0.5 * LOG2E)).astype(BF16)
        for blk in range(nblk):
            qt_ref[blk, rs, :] = xn[:, blk * TBLK:(blk + 1) * TBLK]
    v_t = seg_t(512, 1024).astype(BF16)
    iq_t = seg_t(1024, 1536).astype(BF16)
    iw_t = seg_t(1536, 1552)
    for blk in range(nblk):
        cs = slice(blk * TBLK, (blk + 1) * TBLK)
        _store_values(vt_ref, blk, v_t[:, cs])
        iqt_ref[blk] = iq_t[:, cs]
        iwt_ref[blk] = iw_t[0:IDX_HEADS, cs]
    for h, kh in enumerate(_head_rms(seg(0, 512), kg_ref[...])):
        k_ref[:, h * LANES:(h + 1) * LANES] = kh.astype(BF16)
    ik_ref[...] = seg(512, 640).astype(BF16)
    z_ref[...] = seg(640, 1152)


def _half_rms(y, g2):
    lo = lax.broadcasted_iota(I32, (y.shape[0], LANES), 1) < DIFF_HD
    outs = []
    for h in range(4):
        yh = y[:, h * LANES:(h + 1) * LANES]
        sq = yh * yh
        s_lo = jnp.sum(jnp.where(lo, sq, 0.0), axis=-1, keepdims=True)
        s_hi = jnp.sum(jnp.where(lo, 0.0, sq), axis=-1, keepdims=True)
        ms = jnp.where(lo, s_lo, s_hi) * (1.0 / DIFF_HD)
        outs.append((yh * lax.rsqrt(ms + EPS)) * g2)
    return outs


def _proj_diff_kernel(x_ref, g_ref, wt_ref, w_ref, qg_ref, kg_ref,
                      q1_ref, q2_ref, k_ref, vt_ref, z_ref):
    hb = _rms_rows(x_ref[...], g_ref[...]).astype(BF16)
    nblk = hb.shape[0] // TBLK
    q_t = _nt(wt_ref[0:512, :], hb)
    v_t = _nt(wt_ref[512:1024, :], hb)
    zeros = jnp.zeros((DIFF_HD, TBLK), BF16)
    for j in range(2 * DIFF_HEADS):
        rs = slice(j * DIFF_HD, (j + 1) * DIFF_HD)
        xj = q_t[rs, :]
        ms = jnp.mean(xj * xj, axis=0, keepdims=True)
        xn = (((xj * lax.rsqrt(ms + EPS)) * qg_ref[...]) * (DIFF_HD ** -0.5 * LOG2E)).astype(BF16)
        own, other = (q1_ref, q2_ref) if j % 2 == 0 else (q2_ref, q1_ref)
        for blk in range(nblk):
            own[blk, rs, :] = xn[:, blk * TBLK:(blk + 1) * TBLK]
            other[blk, rs, :] = zeros
    for blk in range(nblk):
        _store_values(vt_ref, blk, v_t[:, blk * TBLK:(blk + 1) * TBLK].astype(BF16))
    ks = _half_rms(jnp.dot(hb, w_ref[:, 0:512], preferred_element_type=F32), kg_ref[...])
    for h, kh in enumerate(ks):
        k_ref[:, h * LANES:(h + 1) * LANES] = kh.astype(BF16)
    z_ref[...] = jnp.dot(hb, w_ref[:, 512:1024], preferred_element_type=F32)


def _row_spec(width, rows=PROJ_ROWS):
    return pl.BlockSpec((rows, width), lambda i: (i, 0))


def _full_spec(shape):
    nd = len(shape)
    return pl.BlockSpec(shape, lambda i: (0,) * nd)


def _row_out(n, width, dtype):
    return jax.ShapeDtypeStruct((n, width), dtype), _row_spec(width)


def _fm_out(n, feat, dtype):
    return (jax.ShapeDtypeStruct((n // TBLK, feat, TBLK), dtype),
            pl.BlockSpec((PROJ_ROWS // TBLK, feat, TBLK), lambda i: (i, 0, 0)))


def _proj_call(kernel, name, x2, g, consts, outs):
    n = x2.shape[0]
    in_specs = [_row_spec(D_MODEL), _full_spec(g.shape)] + [_full_spec(e.shape) for e in consts]
    return pl.pallas_call(
        kernel,
        grid=(n // PROJ_ROWS,),
        in_specs=in_specs,
        out_specs=[spec for _, spec in outs],
        out_shape=[shape for shape, _ in outs],
        compiler_params=_cparams(("parallel",)),
        name=name,
    )(x2, g, *consts)


def _gla_kernel(q_ref, k_ref, v_ref, a_ref, z_ref, wa_ref, ba_ref, ng_ref, o_ref, st_ref):
    rows = q_ref.shape[0]
    nchunk = rows // CHUNK

    @pl.when(pl.program_id(1) == 0)
    def _():
        st_ref[...] = jnp.zeros_like(st_ref)

    pre = jnp.dot(a_ref[...], wa_ref[...], preferred_element_type=F32) + ba_ref[...]
    log_a = -(jnp.maximum(-pre, 0.0) + jnp.log1p(jnp.exp(-jnp.abs(pre)))) / GLA_TAU
    pos = lax.broadcasted_iota(I32, (rows, 1), 0) & (CHUNK - 1)
    cum = log_a
    s = 1
    while s < CHUNK:
        cum = cum + jnp.where(pos >= s, pltpu.roll(cum, s, axis=0), 0.0)
        s *= 2
    lo = lax.broadcasted_iota(I32, (CHUNK, LANES), 1) < GLA_DK

    for c in range(nchunk):
        r0 = c * CHUNK
        total = cum[r0 + CHUNK - 1:r0 + CHUNK, :]
        decay_to_end = jnp.exp(total - cum[r0:r0 + CHUNK, :])
        k_dec = (k_ref[r0:r0 + CHUNK, :].astype(F32) * decay_to_end).astype(BF16)
        a_tot = jnp.exp(total)
        q_c = q_ref[r0:r0 + CHUNK, :].astype(F32) * (GLA_DK ** -0.5)
        for h in range(GLA_HEADS):
            j = h // 2
            pair = slice(j * LANES, (j + 1) * LANES)
            v_h = v_ref[r0:r0 + CHUNK, h * GLA_DV:(h + 1) * GLA_DV]
            u_t = lax.dot_general(v_h, k_dec[:, pair], (((0,), (0,)), ((), ())),
                                  preferred_element_type=F32)
            st = st_ref[h] * a_tot[:, pair] + u_t
            st_ref[h] = st
            keep = lo if h % 2 == 0 else jnp.logical_not(lo)
            q_h = jnp.where(keep, q_c[:, pair], 0.0).astype(BF16)
            o = _nt(q_h, st.astype(BF16))
            ms = jnp.mean(o * o, axis=-1, keepdims=True)
            o = (o * lax.rsqrt(ms + EPS)) * ng_ref[...]
            zh = z_ref[r0:r0 + CHUNK, h * GLA_DV:(h + 1) * GLA_DV]
            o_ref[r0:r0 + CHUNK, h * GLA_DV:(h + 1) * GLA_DV] = (o * _silu(zh)).astype(BF16)


def _gla_call(q, k, v, a, z, wa, ba, ng, b, t):
    nb = t // GLA_ROWS

    def rs(width):
        return pl.BlockSpec((GLA_ROWS, width), lambda bi, i: (bi * nb + i, 0))

    def fs(shape):
        return pl.BlockSpec(shape, lambda bi, i: (0,) * len(shape))

    return pl.pallas_call(
        _gla_kernel,
        grid=(b, nb),
        in_specs=[rs(256), rs(256), rs(512), rs(128), rs(512),
                  fs(wa.shape), fs(ba.shape), fs(ng.shape)],
        out_specs=rs(512),
        out_shape=jax.ShapeDtypeStruct((b * t, BR_W), BF16),
        scratch_shapes=[pltpu.VMEM((GLA_HEADS, GLA_DV, LANES), F32)],
        compiler_params=_cparams(("parallel", "arbitrary")),
        name="gla_mixer",
    )(q, k, v, a, z, wa, ba, ng)


def _softmax_unit(s_ref, p_ref, m_ref, i, allowed=None):
    nk = s_ref.shape[1]

    def piece(r):
        s = s_ref[i, r:r + SUB, :]
        return s if allowed is None else jnp.where(allowed(r), s, NEG)

    mx = None
    for r in range(0, nk, SUB):
        part = _fold(piece(r), jnp.max)
        mx = part if mx is None else jnp.maximum(mx, part)
    m_old = m_ref[i]
    m_new = jnp.maximum(m_old, jnp.max(mx, axis=0, keepdims=True))
    for r in range(0, nk, SUB):
        p_ref[i, r:r + SUB, :] = jnp.exp2((piece(r) - m_new).astype(BF16))
    m_ref[i] = m_new
    return jnp.exp2(m_old - m_new)


def _dsa_kernel(qt_ref, iqt_ref, iwt_ref, z_ref, qg_ref, kg_ref, k_ref, vt_ref, ik_ref, o_ref,
                keys_ref, hi_ref, lo_ref, iqm_ref, d_ref, acc_ref, m_ref, s0_ref, s1_ref, p_ref,
                *, topk, idx_bits):
    nq = qt_ref.shape[1]
    kb_sz = keys_ref.shape[1]
    nhalf = kb_sz // TBLK
    qi = pl.program_id(1)
    q0 = qi * nq
    nkb = (q0 + nq + kb_sz - 1) >> (kb_sz.bit_length() - 1)

    q_end = q0 + ((lax.broadcasted_iota(I32, (1, nq), 1) >> CHUNK_SHIFT) + 1) * CHUNK
    sub_iota = lax.broadcasted_iota(I32, (SUB, nq), 0)

    upper = lax.broadcasted_iota(I32, (LANES, nq), 0) >= IDX_HD
    for h in range(IDX_HEADS):
        pair = iqt_ref[(h // 2) * LANES:(h // 2 + 1) * LANES, :].astype(F32)
        keep = upper if h % 2 else jnp.logical_not(upper)
        iqm_ref[h] = jnp.where(keep, pair, 0.0).astype(BF16)
    wc = iwt_ref[...] * (IDX_HEADS ** -0.5 * IDX_HD ** -0.5)

    def score_body(kb, carry):
        for half in range(nhalf):
            kstart = pl.multiple_of(kb * kb_sz + half * TBLK, TBLK)
            ikb = ik_ref[pl.ds(kstart, TBLK), :]
            for h in range(IDX_HEADS):
                d_ref[h] = jnp.dot(ikb, iqm_ref[h], preferred_element_type=F32)
            for r in range(0, TBLK, SUB):
                s = jnp.zeros((SUB, nq), F32)
                for h in range(IDX_HEADS):
                    s = s + jnp.maximum(d_ref[h, r:r + SUB, :], 0.0) * wc[h:h + 1, :]
                bits = lax.bitcast_convert_type(s, I32)
                sign = bits >> 31
                key = ((bits & 0x7FFFFFFF) ^ sign) - sign
                store_key(kb, half * TBLK + r, key)
        return carry

    def store_key(kb, r, key):
        rs = slice(r, r + SUB)
        keys_ref[kb, rs, :] = key
        hi_ref[kb, rs, :] = (key >> 16).astype(I16)
        lo_ref[kb, rs, :] = ((key & 0xFFFF) + I16_MIN).astype(I16)

    lax.fori_loop(0, nkb, score_body, 0)

    for r in range(0, kb_sz, SUB):
        pos = (nkb - 1) * kb_sz + r + sub_iota
        store_key(nkb - 1, r, jnp.where(pos < q_end, keys_ref[nkb - 1, r:r + SUB, :], INT_MIN))


    def count(pred):
        def body(kb, part):
            for r in range(0, kb_sz, SUB):
                hit = pred(keys_ref[kb, r:r + SUB, :], kb * kb_sz + r)
                part = part + _fold(jnp.where(hit, 1.0, 0.0), jnp.sum)
            return part
        part = lax.fori_loop(0, nkb, body, jnp.zeros((SUBLANES, nq), F32))
        return jnp.sum(part, axis=0, keepdims=True)

    n_acc = 4
    piece = n_acc * PACK * 2
    one16, zero16 = jnp.ones((), I16), jnp.zeros((), I16)

    def count16(plane_ref, pred):
        def body(kb, accs):
            accs = list(accs)
            for r in range(0, kb_sz, piece):
                ones = jnp.where(pred(plane_ref[kb, r:r + piece, :]), one16, zero16)
                for j in range(piece // PACK):
                    accs[j % n_acc] = accs[j % n_acc] + ones[j * PACK:(j + 1) * PACK, :]
            return tuple(accs)
        accs = lax.fori_loop(0, nkb, body, (jnp.zeros((PACK, nq), I16),) * n_acc)
        tot = (accs[0] + accs[1]) + (accs[2] + accs[3])
        return jnp.sum(tot.astype(I32).astype(F32), axis=0, keepdims=True)

    def search16(plane_ref, want, c_start):
        def bit_body(i, carry):
            res, cres = carry
            cand = res | lax.shift_left(jnp.int32(1), 15 - i)
            cand16 = (cand + I16_MIN).astype(I16)
            cnt = count16(plane_ref, lambda kv: kv >= cand16)
            take = cnt >= want
            return jnp.where(take, cand, res), jnp.where(take, cnt, cres)
        return lax.fori_loop(0, 16, bit_body, (jnp.zeros((1, nq), I32), c_start))

    zero_f = jnp.zeros((1, nq), F32)
    hi_u, c_ge_hi = search16(hi_ref, jnp.full((1, nq), float(topk), F32), zero_f)
    hi16 = (hi_u + I16_MIN).astype(I16)
    c_gt_hi = count16(hi_ref, lambda kv: kv > hi16)
    low_min = jnp.full((), I16_MIN, I16)

    def bucket_body(kb, carry):
        for r in range(0, kb_sz, piece):
            rs = slice(r, r + piece)
            lo_ref[kb, rs, :] = jnp.where(hi_ref[kb, rs, :] == hi16, lo_ref[kb, rs, :], low_min)
        return carry

    lax.fori_loop(0, nkb, bucket_body, 0)
    lo_u, c_ge_lo = search16(lo_ref, topk - c_gt_hi, c_ge_hi - c_gt_hi)
    thr = ((hi_u + I16_MIN) << 16) | lo_u
    n_ge = c_gt_hi + c_ge_lo

    excess = jnp.where(jnp.logical_and(n_ge > topk, thr > INT_MIN), 1.0, 0.0)
    thr = jnp.maximum(thr, INT_MIN + 1)

    @pl.when(jnp.max(excess) > 0.0)
    def _():
        need = topk - count(lambda kv, _: kv > thr)

        def idx_body(i, p):
            cand = p | lax.shift_left(jnp.int32(1), idx_bits - 1 - i)
            below = count(lambda kv, r0: jnp.logical_and(kv == thr, r0 + sub_iota < cand))
            return jnp.where(below <= need - 1.0, cand, p)

        last = lax.fori_loop(0, idx_bits, idx_body, jnp.zeros((1, nq), I32))
        excess_b = excess > 0.0

        def demote_body(kb, carry):
            for r in range(0, kb_sz, SUB):
                kv = keys_ref[kb, r:r + SUB, :]
                drop = jnp.logical_and(jnp.logical_and(kv == thr, excess_b),
                                       kb * kb_sz + r + sub_iota > last)
                keys_ref[kb, r:r + SUB, :] = jnp.where(drop, INT_MIN, kv)
            return carry

        lax.fori_loop(0, nkb, demote_body, 0)

    acc_ref[...] = jnp.zeros_like(acc_ref)
    m_ref[...] = jnp.full_like(m_ref, NEG)

    nblk = (q0 + nq) >> (TBLK.bit_length() - 1)
    half_shift = nhalf.bit_length() - 1

    def logits(j, s_ref):
        kstart = pl.multiple_of(j * TBLK, TBLK)
        krows = pl.ds(pl.multiple_of((j & (nhalf - 1)) * TBLK, TBLK), TBLK)
        for h in range(DSA_HEADS):
            hs = slice(h * DSA_HD, (h + 1) * DSA_HD)
            s_t = jnp.dot(k_ref[pl.ds(kstart, TBLK), hs], qt_ref[hs, :],
                          preferred_element_type=F32)
            sel = keys_ref[j >> half_shift, krows, :] >= thr
            s_ref[h] = jnp.where(sel, s_t, NEG)

    def consume(j, s_ref):
        for h in range(DSA_HEADS):
            alpha = _softmax_unit(s_ref, p_ref, m_ref, h)
            acc_ref[h] = alpha * acc_ref[h] + jnp.dot(
                vt_ref[j, h * VROWS:(h + 1) * VROWS, :], p_ref[h],
                preferred_element_type=F32)

    def pair_body(t, carry):
        logits(2 * t + 1, s1_ref)
        consume(2 * t, s0_ref)
        logits(jnp.minimum(2 * t + 2, nblk - 1), s0_ref)
        consume(2 * t + 1, s1_ref)
        return carry

    def unshifted_probs(j, slot):
        kstart = pl.multiple_of(j * TBLK, TBLK)
        krows = pl.ds(pl.multiple_of((j & (nhalf - 1)) * TBLK, TBLK), TBLK)
        for h in range(DSA_HEADS):
            hs = slice(h * DSA_HD, (h + 1) * DSA_HD)
            s_t = jnp.dot(k_ref[pl.ds(kstart, TBLK), hs], qt_ref[hs, :],
                          preferred_element_type=F32)
            sel = keys_ref[j >> half_shift, krows, :] >= thr
            p_ref[slot * DSA_HEADS + h] = jnp.exp2(jnp.where(sel, s_t, NEG)).astype(BF16)

    def weighted_values(j, slot, h):
        return jnp.dot(vt_ref[j, h * VROWS:(h + 1) * VROWS, :], p_ref[slot * DSA_HEADS + h],
                       preferred_element_type=F32)

    def unshifted_block(j, slot):
        unshifted_probs(j, slot)
        for h in range(DSA_HEADS):
            acc_ref[h] += weighted_values(j, slot, h)

    def unshifted_pair(t, carry):
        unshifted_probs(2 * t, 0)
        unshifted_probs(2 * t + 1, 1)
        for h in range(DSA_HEADS):
            acc_ref[h] += weighted_values(2 * t, 0, h) + weighted_values(2 * t + 1, 1, h)
        return carry

    bound = (DSA_HD ** 0.5 * LOG2E * NORM_SLACK) * jnp.max(jnp.abs(qg_ref[...])) * jnp.max(
        jnp.abs(kg_ref[...]))
    small = bound <= MAX_UNSHIFTED_LOG2

    @pl.when(small)
    def _():
        lax.fori_loop(0, nblk >> 1, unshifted_pair, 0)

        @pl.when((nblk & 1) == 1)
        def _():
            unshifted_block(nblk - 1, 0)

    @pl.when(jnp.logical_not(small))
    def _():
        logits(jnp.int32(0), s0_ref)
        lax.fori_loop(0, nblk >> 1, pair_body, 0)

        @pl.when((nblk & 1) == 1)
        def _():
            consume(nblk - 1, s0_ref)

    for h in range(DSA_HEADS):
        hs = slice(h * DSA_HD, (h + 1) * DSA_HD)
        o = (acc_ref[h, 0:HEAD_W, :] / acc_ref[h, HEAD_W:HEAD_W + 1, :]).T
        o_ref[:, hs] = (o * _silu(z_ref[:, hs])).astype(BF16)


def _dsa_call(qt, iqt, iwt, z, qg, kg, k, vt, ik, b, t):
    nq = t // TBLK

    def fs(shape):
        return pl.BlockSpec(shape, lambda bi, i: (0,) * len(shape))

    topk = min(TOPK_MAX, t // 4)
    idx_bits = max(1, (t - 1).bit_length())

    def fm_q(feat):
        return pl.BlockSpec((None, feat, TBLK), lambda bi, i: (bi * nq + i, 0, 0))

    def rows_q(width):
        return pl.BlockSpec((TBLK, width), lambda bi, i: (bi * nq + i, 0))

    def rows_b(width):
        return _resident((t, width), lambda bi, i: (bi, 0))

    kernel = functools.partial(_dsa_kernel, topk=topk, idx_bits=idx_bits)
    return pl.pallas_call(
        kernel,
        grid=(b, nq),
        in_specs=[fm_q(512), fm_q(512), fm_q(IDX_HEADS), rows_q(512), fs(qg.shape), fs(kg.shape),
                  rows_b(512),
                  _resident((nq, 4 * VROWS, TBLK), lambda bi, i: (bi, 0, 0)), rows_b(LANES)],
        out_specs=rows_q(512),
        out_shape=jax.ShapeDtypeStruct((b * t, BR_W), BF16),
        scratch_shapes=[
            pltpu.VMEM((t // DSA_KB, DSA_KB, TBLK), I32),
            pltpu.VMEM((t // DSA_KB, DSA_KB, TBLK), I16),
            pltpu.VMEM((t // DSA_KB, DSA_KB, TBLK), I16),
            pltpu.VMEM((IDX_HEADS, LANES, TBLK), BF16),
            pltpu.VMEM((IDX_HEADS, TBLK, TBLK), F32),
            pltpu.VMEM((DSA_HEADS, VROWS, TBLK), F32),
            pltpu.VMEM((DSA_HEADS, 1, TBLK), F32),
            pltpu.VMEM((DSA_HEADS, TBLK, TBLK), F32),
            pltpu.VMEM((DSA_HEADS, TBLK, TBLK), F32),
            pltpu.VMEM((2 * DSA_HEADS, TBLK, TBLK), BF16),
        ],
        compiler_params=_cparams(("parallel", "arbitrary")),
        name="dsa_mixer",
    )(qt, iqt, iwt, z, qg, kg, k, vt, ik)


def _diff_kernel(q1_ref, q2_ref, z_ref, lq1_ref, lk1_ref, lq2_ref, lk2_ref, qg_ref, kg_ref,
                 k_ref, vt_ref, o_ref, acc_ref, m_ref, s0_ref, s1_ref, p_ref, *, lambda_init):
    nq = q1_ref.shape[1]
    qi = pl.program_id(1)
    lam = (jnp.exp(jnp.sum(lq1_ref[...] * lk1_ref[...], axis=-1, keepdims=True))
           - jnp.exp(jnp.sum(lq2_ref[...] * lk2_ref[...], axis=-1, keepdims=True))
           + lambda_init)

    acc_ref[...] = jnp.zeros_like(acc_ref)
    m_ref[...] = jnp.full_like(m_ref, NEG)

    q_end = ((lax.broadcasted_iota(I32, (1, nq), 1) >> CHUNK_SHIFT) + 1) * CHUNK
    sub_iota = lax.broadcasted_iota(I32, (SUB, nq), 0)

    def logits(kb, s_ref):
        rows = pl.ds(pl.multiple_of(kb * nq, nq), nq)
        for h in range(DIFF_HEADS):
            hs = slice(h * LANES, (h + 1) * LANES)
            k_h = k_ref[rows, hs]
            for c, q_ref in enumerate((q1_ref, q2_ref)):
                s_ref[2 * h + c] = jnp.dot(k_h, q_ref[hs, :], preferred_element_type=F32)

    def consume(kb, s_ref, allowed=None):
        for i in range(2 * DIFF_HEADS):
            vs = slice((i // 2) * VROWS, (i // 2 + 1) * VROWS)
            alpha = _softmax_unit(s_ref, p_ref, m_ref, i, allowed)
            acc_ref[i] = alpha * acc_ref[i] + jnp.dot(vt_ref[kb, vs, :], p_ref[i],
                                                      preferred_element_type=F32)

    def pair_body(t, carry):
        logits(2 * t + 1, s1_ref)
        consume(2 * t, s0_ref)
        logits(2 * t + 2, s0_ref)
        consume(2 * t + 1, s1_ref)
        return carry

    def chunk_mask(r):
        return r + sub_iota < q_end

    units = 2 * DIFF_HEADS

    def unshifted_probs(kb, slot, diagonal):
        rows = pl.ds(pl.multiple_of(kb * nq, nq), nq)
        for h in range(DIFF_HEADS):
            hs = slice(h * LANES, (h + 1) * LANES)
            k_h = k_ref[rows, hs]
            for c, q_ref in enumerate((q1_ref, q2_ref)):
                s_t = jnp.dot(k_h, q_ref[hs, :], preferred_element_type=F32)
                if diagonal:
                    s_t = jnp.where(lax.broadcasted_iota(I32, (nq, nq), 0) < q_end, s_t, NEG)
                p_ref[slot * units + 2 * h + c] = jnp.exp2(s_t).astype(BF16)

    def weighted_values(kb, slot, i):
        vs = slice((i // 2) * VROWS, (i // 2 + 1) * VROWS)
        return jnp.dot(vt_ref[kb, vs, :], p_ref[slot * units + i], preferred_element_type=F32)

    def unshifted_block(kb, slot, diagonal):
        unshifted_probs(kb, slot, diagonal)
        for i in range(units):
            acc_ref[i] += weighted_values(kb, slot, i)

    def unshifted_pair(t, carry):
        unshifted_probs(2 * t, 0, False)
        unshifted_probs(2 * t + 1, 1, False)
        for i in range(units):
            acc_ref[i] += weighted_values(2 * t, 0, i) + weighted_values(2 * t + 1, 1, i)
        return carry

    bound = (DIFF_HD ** 0.5 * LOG2E * NORM_SLACK) * jnp.max(jnp.abs(qg_ref[...])) * jnp.max(
        jnp.abs(kg_ref[...]))
    small = bound <= MAX_UNSHIFTED_LOG2

    @pl.when(small)
    def _():
        lax.fori_loop(0, qi >> 1, unshifted_pair, 0)

        @pl.when((qi & 1) == 1)
        def _():
            unshifted_block(qi - 1, 0, False)

        unshifted_block(qi, 1, True)

    @pl.when(jnp.logical_not(small))
    def _():
        logits(jnp.int32(0), s0_ref)
        lax.fori_loop(0, qi >> 1, pair_body, 0)

        @pl.when((qi & 1) == 0)
        def _():
            consume(qi, s0_ref, chunk_mask)

        @pl.when((qi & 1) == 1)
        def _():
            logits(qi, s1_ref)
            consume(qi - 1, s0_ref)
            consume(qi, s1_ref, chunk_mask)

    def normalised(i):
        return acc_ref[i, 0:HEAD_W, :] / acc_ref[i, HEAD_W:HEAD_W + 1, :]

    for h in range(DIFF_HEADS):
        hs = slice(h * LANES, (h + 1) * LANES)
        o_t = normalised(2 * h) - lam * normalised(2 * h + 1)
        ms = jnp.mean(o_t * o_t, axis=0, keepdims=True)
        o = ((o_t * lax.rsqrt(ms + EPS)) * (1.0 - lambda_init)).T
        o_ref[:, hs] = (o * _silu(z_ref[:, hs])).astype(BF16)


def _diff_call(q1, q2, z, lq1, lk1, lq2, lk2, qg, kg, k, vt, b, t, lambda_init):
    nq = t // TBLK

    def fm_q():
        return pl.BlockSpec((None, 512, TBLK), lambda bi, i: (bi * nq + i, 0, 0))

    def fs(shape):
        return pl.BlockSpec(shape, lambda bi, i: (0,) * len(shape))

    kernel = functools.partial(_diff_kernel, lambda_init=lambda_init)
    return pl.pallas_call(
        kernel,
        grid=(b, nq),
        in_specs=[fm_q(), fm_q(),
                  pl.BlockSpec((TBLK, 512), lambda bi, i: (bi * nq + i, 0)),
                  fs(lq1.shape), fs(lk1.shape), fs(lq2.shape), fs(lk2.shape),
                  fs(qg.shape), fs(kg.shape),
                  _resident((t, 512), lambda bi, i: (bi, 0)),
                  _resident((nq, 4 * VROWS, TBLK), lambda bi, i: (bi, 0, 0))],
        out_specs=pl.BlockSpec((TBLK, 512), lambda bi, i: (bi * nq + i, 0)),
        out_shape=jax.ShapeDtypeStruct((b * t, BR_W), BF16),
        scratch_shapes=[
            pltpu.VMEM((2 * DIFF_HEADS, VROWS, TBLK), F32),
            pltpu.VMEM((2 * DIFF_HEADS, 1, TBLK), F32),
            pltpu.VMEM((2 * DIFF_HEADS, TBLK, TBLK), F32),
            pltpu.VMEM((2 * DIFF_HEADS, TBLK, TBLK), F32),
            pltpu.VMEM((4 * DIFF_HEADS, TBLK, TBLK), BF16),
        ],
        compiler_params=_cparams(("parallel", "arbitrary")),
        name="diff_mixer",
    )(q1, q2, z, lq1, lk1, lq2, lk2, qg, kg, k, vt)


def _out_kernel(x_ref, g_ref, wg_ref, ya_ref, yb_ref, yc_ref, wbr_ref, wo_ref, o_ref):
    x = x_ref[...]
    hb = _rms_rows(x, g_ref[...]).astype(BF16)
    merged = jnp.zeros(x.shape, F32)
    for i, y_ref in enumerate((ya_ref, yb_ref, yc_ref)):
        gate = _sigmoid(jnp.dot(hb, wg_ref[:, i * D_MODEL:(i + 1) * D_MODEL],
                                preferred_element_type=F32))
        merged = merged + gate * jnp.dot(y_ref[...], wbr_ref[i], preferred_element_type=F32)
    o_ref[...] = x + jnp.dot(merged.astype(BF16), wo_ref[...], preferred_element_type=F32)


def _out_call(x2, g, wg, ya, yb, yc, wbr, wo):
    n = x2.shape[0]
    return pl.pallas_call(
        _out_kernel,
        grid=(n // PROJ_ROWS,),
        in_specs=[_row_spec(D_MODEL), _full_spec(g.shape), _full_spec(wg.shape),
                  _row_spec(BR_W), _row_spec(BR_W), _row_spec(BR_W),
                  _full_spec(wbr.shape), _full_spec(wo.shape)],
        out_specs=_row_spec(D_MODEL),
        out_shape=jax.ShapeDtypeStruct((n, D_MODEL), F32),
        compiler_params=_cparams(("parallel",)),
        name="merge_out",
    )(x2, g, wg, ya, yb, yc, wbr, wo)


def _pad_cols(w, width):
    return jnp.pad(w, ((0, 0), (0, width - w.shape[1])))


def kernel(x, norm_g, w_in, gla_wa2, gla_ba, gla_norm_g, dsa_qn_g, dsa_kn_g, diff_qn_g,
           diff_kn_g, diff_lq1, diff_lk1, diff_lq2, diff_lk2, w_br, w_out):
    b, t, d = x.shape
    depth = w_in.shape[0]
    assert d == D_MODEL and t % max(GLA_ROWS, DSA_KB, TBLK) == 0 and (b * t) % PROJ_ROWS == 0
    n = b * t
    x2 = x.reshape(n, d)

    for l in range(depth):
        w = w_in[l]
        seg = [w[:, OFFS[i]:OFFS[i + 1]] for i in range(len(SIZES))]
        (gq, gk, gv, ga, gz, bq, bk, bv, iq, ik, iw, bz, cq, ck, cv, cz, gate) = seg
        w_gla = jnp.concatenate([gq, gk, gv, _pad_cols(ga, LANES), gz], axis=1).astype(BF16)
        wt_dsa = jnp.concatenate([bq, bv, iq, _pad_cols(iw, 16)], axis=1).T.astype(BF16)
        w_dsa = jnp.concatenate([bk, ik, ik, bz], axis=1).astype(BF16)
        wt_diff = jnp.concatenate([cq, cv], axis=1).T.astype(BF16)
        w_diff = jnp.concatenate([ck, cz], axis=1).astype(BF16)
        w_gate = gate.astype(BF16)
        g = norm_g[l].reshape(1, d)

        g_q, g_k, g_v, g_a, g_z = _proj_call(
            _proj_gla_kernel, "proj_gla", x2, g, [w_gla],
            [_row_out(n, 256, BF16), _row_out(n, 256, BF16), _row_out(n, 512, BF16),
             _row_out(n, 128, BF16), _row_out(n, 512, F32)])
        b_qt, b_k, b_vt, i_qt, i_k, i_wt, b_z = _proj_call(
            _proj_dsa_kernel, "proj_dsa", x2, g,
            [wt_dsa, w_dsa, dsa_qn_g[l].reshape(DSA_HD, 1), dsa_kn_g[l].reshape(1, DSA_HD)],
            [_fm_out(n, 512, BF16), _row_out(n, 512, BF16), _fm_out(n, 4 * VROWS, BF16),
             _fm_out(n, 512, BF16), _row_out(n, LANES, BF16), _fm_out(n, IDX_HEADS, F32),
             _row_out(n, 512, F32)])
        c_q1, c_q2, c_k, c_vt, c_z = _proj_call(
            _proj_diff_kernel, "proj_diff", x2, g,
            [wt_diff, w_diff, diff_qn_g[l].reshape(DIFF_HD, 1),
             jnp.tile(diff_kn_g[l], 2).reshape(1, LANES)],
            [_fm_out(n, 512, BF16), _fm_out(n, 512, BF16), _row_out(n, 512, BF16),
             _fm_out(n, 4 * VROWS, BF16), _row_out(n, 512, F32)])

        wa = jnp.pad(gla_wa2[l], ((0, LANES - GLA_RANK), (0, 0))).astype(BF16)
        y_a = _gla_call(g_q, g_k, g_v, g_a, g_z, wa, gla_ba[l].reshape(1, -1),
                        gla_norm_g[l].reshape(1, GLA_DV), b, t)
        y_b = _dsa_call(b_qt, i_qt, i_wt, b_z, dsa_qn_g[l].reshape(1, DSA_HD),
                        dsa_kn_g[l].reshape(1, DSA_HD), b_k, b_vt, i_k, b, t)
        lambda_init = 0.8 - 0.6 * math.exp(-0.3 * l)
        y_c = _diff_call(c_q1, c_q2, c_z, diff_lq1[l].reshape(1, -1), diff_lk1[l].reshape(1, -1),
                         diff_lq2[l].reshape(1, -1), diff_lk2[l].reshape(1, -1),
                         diff_qn_g[l].reshape(1, -1), diff_kn_g[l].reshape(1, -1), c_k, c_vt,
                         b, t, lambda_init)
        x2 = _out_call(x2, g, w_gate, y_a, y_b, y_c, w_br[l].astype(BF16),
                       w_out[l].astype(BF16))
    return x2.reshape(b, t, d)
```

```python
import functools
import math

import jax
import jax.numpy as jnp
from jax import lax
from jax.experimental import pallas as pl
from jax.experimental.pallas import tpu as pltpu

F32 = jnp.float32
BF16 = jnp.bfloat16
I32 = jnp.int32
I16 = jnp.int16

D_MODEL = 1024
CHUNK = 64
CHUNK_SHIFT = 6
EPS = 1e-6
LANES = 128
SUBLANES = 8

GLA_HEADS, GLA_DK, GLA_DV, GLA_RANK, GLA_TAU = 4, 64, 128, 16, 16.0
DSA_HEADS, DSA_HD, IDX_HEADS, IDX_HD, TOPK_MAX = 4, 128, 8, 64, 256
DIFF_HEADS, DIFF_HD = 4, 64
BR_W, N_BRANCH = 512, 3

SIZES = [
    GLA_HEADS * GLA_DK, GLA_HEADS * GLA_DK, GLA_HEADS * GLA_DV, GLA_RANK, BR_W,
    DSA_HEADS * DSA_HD, DSA_HEADS * DSA_HD, DSA_HEADS * DSA_HD,
    IDX_HEADS * IDX_HD, IDX_HD, IDX_HEADS, BR_W,
    DIFF_HEADS * 2 * DIFF_HD, DIFF_HEADS * 2 * DIFF_HD, DIFF_HEADS * 2 * DIFF_HD, BR_W,
    N_BRANCH * D_MODEL,
]
OFFS = [0]
for _s in SIZES:
    OFFS.append(OFFS[-1] + _s)

INT_MIN = -(2 ** 31)
NEG = -1e30
LOG2E = math.log2(math.e)
NORM_SLACK = 1.02
MAX_UNSHIFTED_LOG2 = 60.0

PROJ_ROWS = 512
GLA_ROWS = 512
TBLK = 256
DSA_KB = 512
ATT_GROUP = 4
SUB = 64
PACK = 16
HEAD_W = 128
VROWS = HEAD_W + PACK
I16_MIN = -(2 ** 15)
VMEM_LIMIT = 56 * 1024 * 1024


def _cparams(sem):
    return pltpu.CompilerParams(dimension_semantics=sem, vmem_limit_bytes=VMEM_LIMIT)


def _resident(shape, index_map):
    return pl.BlockSpec(shape, index_map, pipeline_mode=pl.Buffered(1))


def _sigmoid(x):
    return 1.0 / (1.0 + jnp.exp(-x))


def _silu(x):
    return x * _sigmoid(x)


def _rms_rows(x, g):
    ms = jnp.mean(x * x, axis=-1, keepdims=True)
    return (x * lax.rsqrt(ms + EPS)) * g


def _nt(a, b):
    return lax.dot_general(a, b, (((1,), (1,)), ((), ())), preferred_element_type=F32)


def _fold(x, op):
    return op(x.reshape(x.shape[0] // SUBLANES, SUBLANES, x.shape[1]), axis=0)


def _proj_gla_kernel(x_ref, g_ref, w_ref, q_ref, k_ref, v_ref, a_ref, z_ref):
    hb = _rms_rows(x_ref[...], g_ref[...]).astype(BF16)

    def seg(a, b):
        return jnp.dot(hb, w_ref[:, a:b], preferred_element_type=F32)

    q_ref[...] = seg(0, 256).astype(BF16)
    k_ref[...] = seg(256, 512).astype(BF16)
    v_ref[...] = seg(512, 1024).astype(BF16)
    a_ref[...] = seg(1024, 1152).astype(BF16)
    z_ref[...] = seg(1152, 1664)


def _head_rms(y, g):
    outs = []
    for h in range(4):
        yh = y[:, h * LANES:(h + 1) * LANES]
        ms = jnp.mean(yh * yh, axis=-1, keepdims=True)
        outs.append((yh * lax.rsqrt(ms + EPS)) * g)
    return outs


def _store_values(vt_ref, blk, v_t):
    ones = jnp.ones((PACK, v_t.shape[1]), BF16)
    for h in range(4):
        vt_ref[blk, h * VROWS:h * VROWS + HEAD_W, :] = v_t[h * HEAD_W:(h + 1) * HEAD_W, :]
        vt_ref[blk, h * VROWS + HEAD_W:(h + 1) * VROWS, :] = ones


def _proj_dsa_kernel(x_ref, g_ref, wt_ref, w_ref, qg_ref, kg_ref,
                     qt_ref, k_ref, vt_ref, iqt_ref, ik_ref, iwt_ref, z_ref):
    hb = _rms_rows(x_ref[...], g_ref[...]).astype(BF16)
    nblk = hb.shape[0] // TBLK

    def seg(a, b):
        return jnp.dot(hb, w_ref[:, a:b], preferred_element_type=F32)

    def seg_t(a, b):
        return _nt(wt_ref[a:b, :], hb)

    q_t = seg_t(0, 512)
    for h in range(DSA_HEADS):
        rs = slice(h * DSA_HD, (h + 1) * DSA_HD)
        xh = q_t[rs, :]
        ms = jnp.mean(xh * xh, axis=0, keepdims=True)
        xn = (((xh * lax.rsqrt(ms + EPS)) * qg_ref[...]) * (DSA_HD ** -0.5 * LOG2E)).astype(BF16)
        for blk in range(nblk):
            qt_ref[blk, rs, :] = xn[:, blk * TBLK:(blk + 1) * TBLK]
    v_t = seg_t(512, 1024).astype(BF16)
    iq_t = seg_t(1024, 1536).astype(BF16)
    iw_t = seg_t(1536, 1552)
    for blk in range(nblk):
        cs = slice(blk * TBLK, (blk + 1) * TBLK)
        _store_values(vt_ref, blk, v_t[:, cs])
        iqt_ref[blk] = iq_t[:, cs]
        iwt_ref[blk] = iw_t[0:IDX_HEADS, cs]
    for h, kh in enumerate(_head_rms(seg(0, 512), kg_ref[...])):
        k_ref[:, h * LANES:(h + 1) * LANES] = kh.astype(BF16)
    ik_ref[...] = seg(512, 640).astype(BF16)
    z_ref[...] = seg(640, 1152)


def _half_rms(y, g2):
    lo = lax.broadcasted_iota(I32, (y.shape[0], LANES), 1) < DIFF_HD
    outs = []
    for h in range(4):
        yh = y[:, h * LANES:(h + 1) * LANES]
        sq = yh * yh
        s_lo = jnp.sum(jnp.where(lo, sq, 0.0), axis=-1, keepdims=True)
        s_hi = jnp.sum(jnp.where(lo, 0.0, sq), axis=-1, keepdims=True)
        ms = jnp.where(lo, s_lo, s_hi) * (1.0 / DIFF_HD)
        outs.append((yh * lax.rsqrt(ms + EPS)) * g2)
    return outs


def _proj_diff_kernel(x_ref, g_ref, wt_ref, w_ref, qg_ref, kg_ref,
                      q1_ref, q2_ref, k_ref, vt_ref, z_ref):
    hb = _rms_rows(x_ref[...], g_ref[...]).astype(BF16)
    nblk = hb.shape[0] // TBLK
    q_t = _nt(wt_ref[0:512, :], hb)
    v_t = _nt(wt_ref[512:1024, :], hb)
    zeros = jnp.zeros((DIFF_HD, TBLK), BF16)
    for j in range(2 * DIFF_HEADS):
        rs = slice(j * DIFF_HD, (j + 1) * DIFF_HD)
        xj = q_t[rs, :]
        ms = jnp.mean(xj * xj, axis=0, keepdims=True)
        xn = (((xj * lax.rsqrt(ms + EPS)) * qg_ref[...]) * (DIFF_HD ** -0.5 * LOG2E)).astype(BF16)
        own, other = (q1_ref, q2_ref) if j % 2 == 0 else (q2_ref, q1_ref)
        for blk in range(nblk):
            own[blk, rs, :] = xn[:, blk * TBLK:(blk + 1) * TBLK]
            other[blk, rs, :] = zeros
    for blk in range(nblk):
        _store_values(vt_ref, blk, v_t[:, blk * TBLK:(blk + 1) * TBLK].astype(BF16))
    ks = _half_rms(jnp.dot(hb, w_ref[:, 0:512], preferred_element_type=F32), kg_ref[...])
    for h, kh in enumerate(ks):
        k_ref[:, h * LANES:(h + 1) * LANES] = kh.astype(BF16)
    z_ref[...] = jnp.dot(hb, w_ref[:, 512:1024], preferred_element_type=F32)


def _row_spec(width, rows=PROJ_ROWS):
    return pl.BlockSpec((rows, width), lambda i: (i, 0))


def _full_spec(shape):
    nd = len(shape)
    return pl.BlockSpec(shape, lambda i: (0,) * nd)


def _row_out(n, width, dtype):
    return jax.ShapeDtypeStruct((n, width), dtype), _row_spec(width)


def _fm_out(n, feat, dtype):
    return (jax.ShapeDtypeStruct((n // TBLK, feat, TBLK), dtype),
            pl.BlockSpec((PROJ_ROWS // TBLK, feat, TBLK), lambda i: (i, 0, 0)))


def _proj_call(kernel, name, x2, g, consts, outs):
    n = x2.shape[0]
    in_specs = [_row_spec(D_MODEL), _full_spec(g.shape)] + [_full_spec(e.shape) for e in consts]
    return pl.pallas_call(
        kernel,
        grid=(n // PROJ_ROWS,),
        in_specs=in_specs,
        out_specs=[spec for _, spec in outs],
        out_shape=[shape for shape, _ in outs],
        compiler_params=_cparams(("parallel",)),
        name=name,
    )(x2, g, *consts)


def _gla_kernel(q_ref, k_ref, v_ref, a_ref, z_ref, wa_ref, ba_ref, ng_ref, o_ref, st_ref):
    nbatch, rows = q_ref.shape[0], q_ref.shape[1]
    nchunk = rows // CHUNK

    @pl.when(pl.program_id(0) == 0)
    def _():
        st_ref[...] = jnp.zeros_like(st_ref)

    pos = lax.broadcasted_iota(I32, (rows, 1), 0) & (CHUNK - 1)
    lo = lax.broadcasted_iota(I32, (CHUNK, LANES), 1) < GLA_DK
    cums = []
    for bb in range(nbatch):
        pre = jnp.dot(a_ref[bb], wa_ref[...], preferred_element_type=F32) + ba_ref[...]
        cum = -(jnp.maximum(-pre, 0.0) + jnp.log1p(jnp.exp(-jnp.abs(pre)))) / GLA_TAU
        s = 1
        while s < CHUNK:
            cum = cum + jnp.where(pos >= s, pltpu.roll(cum, s, axis=0), 0.0)
            s *= 2
        cums.append(cum)

    for c in range(nchunk):
        r0 = c * CHUNK
        for bb in range(nbatch):
            cum = cums[bb]
            total = cum[r0 + CHUNK - 1:r0 + CHUNK, :]
            decay_to_end = jnp.exp(total - cum[r0:r0 + CHUNK, :])
            k_dec = (k_ref[bb, r0:r0 + CHUNK, :].astype(F32) * decay_to_end).astype(BF16)
            a_tot = jnp.exp(total)
            q_c = q_ref[bb, r0:r0 + CHUNK, :].astype(F32) * (GLA_DK ** -0.5)
            for h in range(GLA_HEADS):
                j = h // 2
                pair = slice(j * LANES, (j + 1) * LANES)
                hs = slice(h * GLA_DV, (h + 1) * GLA_DV)
                u_t = lax.dot_general(v_ref[bb, r0:r0 + CHUNK, hs], k_dec[:, pair],
                                      (((0,), (0,)), ((), ())),
                                      preferred_element_type=F32)
                st = st_ref[bb * GLA_HEADS + h] * a_tot[:, pair] + u_t
                st_ref[bb * GLA_HEADS + h] = st
                keep = lo if h % 2 == 0 else jnp.logical_not(lo)
                q_h = jnp.where(keep, q_c[:, pair], 0.0).astype(BF16)
                o = _nt(q_h, st.astype(BF16))
                ms = jnp.mean(o * o, axis=-1, keepdims=True)
                o = (o * lax.rsqrt(ms + EPS)) * ng_ref[...]
                zh = z_ref[bb, r0:r0 + CHUNK, hs]
                o_ref[bb, r0:r0 + CHUNK, hs] = (o * _silu(zh)).astype(BF16)


def _gla_call(q, k, v, a, z, wa, ba, ng, b, t):
    nb = t // GLA_ROWS

    def rs(width):
        return pl.BlockSpec((b, GLA_ROWS, width), lambda i: (0, i, 0))

    def fs(shape):
        return pl.BlockSpec(shape, lambda i: (0,) * len(shape))

    def by_batch(x):
        return x.reshape(b, t, x.shape[-1])

    out = pl.pallas_call(
        _gla_kernel,
        grid=(nb,),
        in_specs=[rs(256), rs(256), rs(512), rs(128), rs(512),
                  fs(wa.shape), fs(ba.shape), fs(ng.shape)],
        out_specs=rs(512),
        out_shape=jax.ShapeDtypeStruct((b, t, BR_W), BF16),
        scratch_shapes=[pltpu.VMEM((b * GLA_HEADS, GLA_DV, LANES), F32)],
        compiler_params=_cparams(("arbitrary",)),
        name="gla_mixer",
    )(by_batch(q), by_batch(k), by_batch(v), by_batch(a), by_batch(z), wa, ba, ng)
    return out.reshape(b * t, BR_W)


def _softmax_unit(s_ref, p_ref, m_ref, i, allowed=None):
    nk = s_ref.shape[1]

    def piece(r):
        s = s_ref[i, r:r + SUB, :]
        return s if allowed is None else jnp.where(allowed(r), s, NEG)

    mx = None
    for r in range(0, nk, SUB):
        part = _fold(piece(r), jnp.max)
        mx = part if mx is None else jnp.maximum(mx, part)
    m_old = m_ref[i]
    m_new = jnp.maximum(m_old, jnp.max(mx, axis=0, keepdims=True))
    for r in range(0, nk, SUB):
        p_ref[i, r:r + SUB, :] = jnp.exp2((piece(r) - m_new).astype(BF16))
    m_ref[i] = m_new
    return jnp.exp2(m_old - m_new)


def _dsa_kernel(qt_ref, iqt_ref, iwt_ref, z_ref, qg_ref, kg_ref, k_ref, vt_ref, ik_ref, o_ref,
                sc_ref, hi_ref, lo_ref, tau_ref, nge_ref, iqm_ref, d_ref, acc_ref, m_ref,
                s0_ref, s1_ref, p_ref, *, topk, idx_bits):
    nq = qt_ref.shape[1]
    kb_sz = sc_ref.shape[1]
    nhalf = kb_sz // TBLK
    qi = pl.program_id(1)
    q0 = qi * nq
    nkb = (q0 + nq + kb_sz - 1) >> (kb_sz.bit_length() - 1)

    q_end = q0 + ((lax.broadcasted_iota(I32, (1, nq), 1) >> CHUNK_SHIFT) + 1) * CHUNK
    sub_iota = lax.broadcasted_iota(I32, (SUB, nq), 0)

    upper = lax.broadcasted_iota(I32, (LANES, nq), 0) >= IDX_HD
    for h in range(IDX_HEADS):
        pair = iqt_ref[(h // 2) * LANES:(h // 2 + 1) * LANES, :].astype(F32)
        keep = upper if h % 2 else jnp.logical_not(upper)
        iqm_ref[h] = jnp.where(keep, pair, 0.0).astype(BF16)
    wc = iwt_ref[...] * (IDX_HEADS ** -0.5 * IDX_HD ** -0.5)

    assert nhalf == 2
    n_sub = nkb * nhalf

    def head_dots(j, half):
        ikb = ik_ref[pl.ds(pl.multiple_of(j * TBLK, TBLK), TBLK), :]
        for h in range(IDX_HEADS):
            d_ref[half * IDX_HEADS + h] = jnp.dot(ikb, iqm_ref[h], preferred_element_type=F32)

    def combine(kb, half):
        d0 = half * IDX_HEADS
        for r in range(0, TBLK, SUB):
            s = jnp.zeros((SUB, nq), F32)
            for h in range(IDX_HEADS):
                s = s + jnp.maximum(d_ref[d0 + h, r:r + SUB, :], 0.0) * wc[h:h + 1, :]
            store_scores(kb, half * TBLK + r, s, None)

    def score_pair(t, carry):
        head_dots(4 * t + 1, 1)
        combine(2 * t, 0)
        head_dots(4 * t + 2, 0)
        combine(2 * t, 1)
        head_dots(4 * t + 3, 1)
        combine(2 * t + 1, 0)
        head_dots(jnp.minimum(4 * t + 4, n_sub - 1), 0)
        combine(2 * t + 1, 1)
        return carry

    def store_scores(kb, r, s, visible):
        rs = slice(r, r + SUB)
        bits = lax.bitcast_convert_type(s, I32)
        sign = bits >> 31
        key = ((bits & 0x7FFFFFFF) ^ sign) - sign
        if visible is not None:
            s = jnp.where(visible, s, -jnp.inf)
            key = jnp.where(visible, key, INT_MIN)
        sc_ref[kb, rs, :] = s
        hi_ref[kb, rs, :] = (key >> 16).astype(I16)
        lo_ref[kb, rs, :] = ((key & 0xFFFF) + I16_MIN).astype(I16)

    head_dots(jnp.int32(0), 0)
    lax.fori_loop(0, nkb >> 1, score_pair, 0)

    @pl.when((nkb & 1) == 1)
    def _():
        head_dots(n_sub - 1, 1)
        combine(nkb - 1, 0)
        combine(nkb - 1, 1)

    for r in range(0, kb_sz, SUB):
        pos = (nkb - 1) * kb_sz + r + sub_iota
        store_scores(nkb - 1, r, sc_ref[nkb - 1, r:r + SUB, :], pos < q_end)


    def count(*preds):
        def body(kb, parts):
            parts = list(parts)
            for r in range(0, kb_sz, SUB):
                sv = sc_ref[kb, r:r + SUB, :]
                for n, pred in enumerate(preds):
                    hit = pred(sv, kb * kb_sz + r)
                    parts[n] = parts[n] + _fold(jnp.where(hit, 1.0, 0.0), jnp.sum)
            return tuple(parts)
        parts = lax.fori_loop(0, nkb, body, (jnp.zeros((SUBLANES, nq), F32),) * len(preds))
        sums = [jnp.sum(part, axis=0, keepdims=True) for part in parts]
        return sums[0] if len(sums) == 1 else sums

    n_acc = 4
    piece = n_acc * PACK * 2
    one16, zero16 = jnp.ones((), I16), jnp.zeros((), I16)

    def count16(plane_ref, pred):
        def body(kb, accs):
            accs = list(accs)
            for r in range(0, kb_sz, piece):
                ones = jnp.where(pred(plane_ref[kb, r:r + piece, :]), one16, zero16)
                for j in range(piece // PACK):
                    accs[j % n_acc] = accs[j % n_acc] + ones[j * PACK:(j + 1) * PACK, :]
            return tuple(accs)
        accs = lax.fori_loop(0, nkb, body, (jnp.zeros((PACK, nq), I16),) * n_acc)
        tot = (accs[0] + accs[1]) + (accs[2] + accs[3])
        return jnp.sum(tot.astype(I32).astype(F32), axis=0, keepdims=True)

    def search16(plane_ref, want, c_start):
        def bit_body(i, carry):
            res, cres = carry
            cand = res | lax.shift_left(jnp.int32(1), 15 - i)
            cand16 = (cand + I16_MIN).astype(I16)
            cnt = count16(plane_ref, lambda kv: kv >= cand16)
            take = cnt >= want
            return jnp.where(take, cand, res), jnp.where(take, cnt, cres)
        return lax.fori_loop(0, 16, bit_body, (jnp.zeros((1, nq), I32), c_start))

    zero_f = jnp.zeros((1, nq), F32)
    hi_u, c_ge_hi = search16(hi_ref, jnp.full((1, nq), float(topk), F32), zero_f)
    hi16 = (hi_u + I16_MIN).astype(I16)
    c_gt_hi = count16(hi_ref, lambda kv: kv > hi16)
    low_min = jnp.full((), I16_MIN, I16)

    def bucket_body(kb, carry):
        for r in range(0, kb_sz, piece):
            rs = slice(r, r + piece)
            lo_ref[kb, rs, :] = jnp.where(hi_ref[kb, rs, :] == hi16, lo_ref[kb, rs, :], low_min)
        return carry

    lax.fori_loop(0, nkb, bucket_body, 0)
    lo_u, _ = search16(lo_ref, topk - c_gt_hi, c_ge_hi - c_gt_hi)
    thr_key = ((hi_u + I16_MIN) << 16) | lo_u

    def key_to_score(key):
        bits = jnp.where(key < 0, (-key) | INT_MIN, key)
        return lax.bitcast_convert_type(bits, F32)

    few = q_end <= topk
    lowest = float(jnp.finfo(F32).min)
    tau0 = jnp.where(few, lowest, key_to_score(thr_key))

    n_ge0, n_gt0 = count(lambda sv, _: sv >= tau0, lambda sv, _: sv > tau0)
    confirmed = jnp.logical_or(few, jnp.logical_and(n_ge0 >= topk, n_gt0 < topk))
    tau_ref[...] = tau0
    nge_ref[...] = n_ge0

    @pl.when(jnp.min(jnp.where(confirmed, 1.0, 0.0)) < 1.0)
    def _():
        def bit_body(i, carry):
            res, cres = carry
            cand = res | lax.shift_left(jnp.int32(1), 31 - i)
            cand_f = key_to_score(cand ^ INT_MIN)
            cnt = count(lambda sv, _: sv >= cand_f)
            take = cnt >= topk
            return jnp.where(take, cand, res), jnp.where(take, cnt, cres)

        res, cres = lax.fori_loop(0, 32, bit_body,
                                  (jnp.zeros((1, nq), I32), jnp.zeros((1, nq), F32)))
        tau_ref[...] = jnp.where(few, lowest, key_to_score(res ^ INT_MIN))
        nge_ref[...] = cres

    tau = tau_ref[...]

    excess = jnp.where(jnp.logical_and(nge_ref[...] > topk, jnp.logical_not(few)), 1.0, 0.0)

    @pl.when(jnp.max(excess) > 0.0)
    def _():
        need = topk - count(lambda sv, _: sv > tau)

        def idx_body(i, p):
            cand = p | lax.shift_left(jnp.int32(1), idx_bits - 1 - i)
            below = count(lambda sv, r0: jnp.logical_and(sv == tau, r0 + sub_iota < cand))
            return jnp.where(below <= need - 1.0, cand, p)

        last = lax.fori_loop(0, idx_bits, idx_body, jnp.zeros((1, nq), I32))
        excess_b = excess > 0.0

        def demote_body(kb, carry):
            for r in range(0, kb_sz, SUB):
                sv = sc_ref[kb, r:r + SUB, :]
                drop = jnp.logical_and(jnp.logical_and(sv == tau, excess_b),
                                       kb * kb_sz + r + sub_iota > last)
                sc_ref[kb, r:r + SUB, :] = jnp.where(drop, -jnp.inf, sv)
            return carry

        lax.fori_loop(0, nkb, demote_body, 0)

    acc_ref[...] = jnp.zeros_like(acc_ref)
    m_ref[...] = jnp.full_like(m_ref, NEG)

    nblk = (q0 + nq) >> (TBLK.bit_length() - 1)
    half_shift = nhalf.bit_length() - 1

    def logits(j, s_ref):
        kstart = pl.multiple_of(j * TBLK, TBLK)
        krows = pl.ds(pl.multiple_of((j & (nhalf - 1)) * TBLK, TBLK), TBLK)
        for h in range(DSA_HEADS):
            hs = slice(h * DSA_HD, (h + 1) * DSA_HD)
            s_t = jnp.dot(k_ref[pl.ds(kstart, TBLK), hs], qt_ref[hs, :],
                          preferred_element_type=F32)
            sel = sc_ref[j >> half_shift, krows, :] >= tau
            s_ref[h] = jnp.where(sel, s_t, NEG)

    def consume(j, s_ref):
        for h in range(DSA_HEADS):
            alpha = _softmax_unit(s_ref, p_ref, m_ref, h)
            acc_ref[h] = alpha * acc_ref[h] + jnp.dot(
                vt_ref[j, h * VROWS:(h + 1) * VROWS, :], p_ref[h],
                preferred_element_type=F32)

    def pair_body(t, carry):
        logits(2 * t + 1, s1_ref)
        consume(2 * t, s0_ref)
        logits(jnp.minimum(2 * t + 2, nblk - 1), s0_ref)
        consume(2 * t + 1, s1_ref)
        return carry

    def unshifted_probs(j, slot):
        kstart = pl.multiple_of(j * TBLK, TBLK)
        krows = pl.ds(pl.multiple_of((j & (nhalf - 1)) * TBLK, TBLK), TBLK)
        for h in range(DSA_HEADS):
            hs = slice(h * DSA_HD, (h + 1) * DSA_HD)
            s_t = jnp.dot(k_ref[pl.ds(kstart, TBLK), hs], qt_ref[hs, :],
                          preferred_element_type=F32)
            sel = sc_ref[j >> half_shift, krows, :] >= tau
            p_ref[slot * DSA_HEADS + h] = jnp.exp2(jnp.where(sel, s_t, NEG)).astype(BF16)

    def weighted_values(j, slot, h):
        return jnp.dot(vt_ref[j, h * VROWS:(h + 1) * VROWS, :], p_ref[slot * DSA_HEADS + h],
                       preferred_element_type=F32)

    def unshifted_blocks(j0, n):
        for s in range(n):
            unshifted_probs(j0 + s, s)
        for h in range(DSA_HEADS):
            total = weighted_values(j0, 0, h)
            for s in range(1, n):
                total = total + weighted_values(j0 + s, s, h)
            acc_ref[h] += total

    def unshifted_group(t, carry):
        unshifted_blocks(ATT_GROUP * t, ATT_GROUP)
        return carry

    bound = (DSA_HD ** 0.5 * LOG2E * NORM_SLACK) * jnp.max(jnp.abs(qg_ref[...])) * jnp.max(
        jnp.abs(kg_ref[...]))
    small = bound <= MAX_UNSHIFTED_LOG2

    @pl.when(small)
    def _():
        ngroup = nblk >> (ATT_GROUP.bit_length() - 1)
        lax.fori_loop(0, ngroup, unshifted_group, 0)
        done = ngroup * ATT_GROUP

        @pl.when((nblk & 2) != 0)
        def _():
            unshifted_blocks(done, 2)

        @pl.when((nblk & 1) != 0)
        def _():
            unshifted_blocks(nblk - 1, 1)

    @pl.when(jnp.logical_not(small))
    def _():
        logits(jnp.int32(0), s0_ref)
        lax.fori_loop(0, nblk >> 1, pair_body, 0)

        @pl.when((nblk & 1) == 1)
        def _():
            consume(nblk - 1, s0_ref)

    for h in range(DSA_HEADS):
        hs = slice(h * DSA_HD, (h + 1) * DSA_HD)
        o = (acc_ref[h, 0:HEAD_W, :] / acc_ref[h, HEAD_W:HEAD_W + 1, :]).T
        o_ref[:, hs] = (o * _silu(z_ref[:, hs])).astype(BF16)


def _dsa_call(qt, iqt, iwt, z, qg, kg, k, vt, ik, b, t):
    nq = t // TBLK

    def fs(shape):
        return pl.BlockSpec(shape, lambda bi, i: (0,) * len(shape))

    topk = min(TOPK_MAX, t // 4)
    idx_bits = max(1, (t - 1).bit_length())

    def fm_q(feat):
        return pl.BlockSpec((None, feat, TBLK), lambda bi, i: (bi * nq + i, 0, 0))

    def rows_q(width):
        return pl.BlockSpec((TBLK, width), lambda bi, i: (bi * nq + i, 0))

    def rows_b(width):
        return _resident((t, width), lambda bi, i: (bi, 0))

    kernel = functools.partial(_dsa_kernel, topk=topk, idx_bits=idx_bits)
    return pl.pallas_call(
        kernel,
        grid=(b, nq),
        in_specs=[fm_q(512), fm_q(512), fm_q(IDX_HEADS), rows_q(512), fs(qg.shape), fs(kg.shape),
                  rows_b(512),
                  _resident((nq, 4 * VROWS, TBLK), lambda bi, i: (bi, 0, 0)), rows_b(LANES)],
        out_specs=rows_q(512),
        out_shape=jax.ShapeDtypeStruct((b * t, BR_W), BF16),
        scratch_shapes=[
            pltpu.VMEM((t // DSA_KB, DSA_KB, TBLK), F32),
            pltpu.VMEM((t // DSA_KB, DSA_KB, TBLK), I16),
            pltpu.VMEM((t // DSA_KB, DSA_KB, TBLK), I16),
            pltpu.VMEM((1, TBLK), F32),
            pltpu.VMEM((1, TBLK), F32),
            pltpu.VMEM((IDX_HEADS, LANES, TBLK), BF16),
            pltpu.VMEM((DSA_KB // TBLK * IDX_HEADS, TBLK, TBLK), F32),
            pltpu.VMEM((DSA_HEADS, VROWS, TBLK), F32),
            pltpu.VMEM((DSA_HEADS, 1, TBLK), F32),
            pltpu.VMEM((DSA_HEADS, TBLK, TBLK), F32),
            pltpu.VMEM((DSA_HEADS, TBLK, TBLK), F32),
            pltpu.VMEM((ATT_GROUP * DSA_HEADS, TBLK, TBLK), BF16),
        ],
        compiler_params=_cparams(("parallel", "arbitrary")),
        name="dsa_mixer",
    )(qt, iqt, iwt, z, qg, kg, k, vt, ik)


def _diff_kernel(q1_ref, q2_ref, z_ref, lq1_ref, lk1_ref, lq2_ref, lk2_ref, qg_ref, kg_ref,
                 k_ref, vt_ref, o_ref, acc_ref, m_ref, s0_ref, s1_ref, p_ref, *, lambda_init):
    nq = q1_ref.shape[1]
    qi = pl.program_id(1)
    lam = (jnp.exp(jnp.sum(lq1_ref[...] * lk1_ref[...], axis=-1, keepdims=True))
           - jnp.exp(jnp.sum(lq2_ref[...] * lk2_ref[...], axis=-1, keepdims=True))
           + lambda_init)

    acc_ref[...] = jnp.zeros_like(acc_ref)
    m_ref[...] = jnp.full_like(m_ref, NEG)

    q_end = ((lax.broadcasted_iota(I32, (1, nq), 1) >> CHUNK_SHIFT) + 1) * CHUNK
    sub_iota = lax.broadcasted_iota(I32, (SUB, nq), 0)

    def logits(kb, s_ref):
        rows = pl.ds(pl.multiple_of(kb * nq, nq), nq)
        for h in range(DIFF_HEADS):
            hs = slice(h * LANES, (h + 1) * LANES)
            k_h = k_ref[rows, hs]
            for c, q_ref in enumerate((q1_ref, q2_ref)):
                s_ref[2 * h + c] = jnp.dot(k_h, q_ref[hs, :], preferred_element_type=F32)

    def consume(kb, s_ref, allowed=None):
        for i in range(2 * DIFF_HEADS):
            vs = slice((i // 2) * VROWS, (i // 2 + 1) * VROWS)
            alpha = _softmax_unit(s_ref, p_ref, m_ref, i, allowed)
            acc_ref[i] = alpha * acc_ref[i] + jnp.dot(vt_ref[kb, vs, :], p_ref[i],
                                                      preferred_element_type=F32)

    def pair_body(t, carry):
        logits(2 * t + 1, s1_ref)
        consume(2 * t, s0_ref)
        logits(2 * t + 2, s0_ref)
        consume(2 * t + 1, s1_ref)
        return carry

    def chunk_mask(r):
        return r + sub_iota < q_end

    units = 2 * DIFF_HEADS

    def unshifted_probs(kb, slot, diagonal):
        rows = pl.ds(pl.multiple_of(kb * nq, nq), nq)
        for h in range(DIFF_HEADS):
            hs = slice(h * LANES, (h + 1) * LANES)
            k_h = k_ref[rows, hs]
            for c, q_ref in enumerate((q1_ref, q2_ref)):
                s_t = jnp.dot(k_h, q_ref[hs, :], preferred_element_type=F32)
                if diagonal:
                    s_t = jnp.where(lax.broadcasted_iota(I32, (nq, nq), 0) < q_end, s_t, NEG)
                p_ref[slot * units + 2 * h + c] = jnp.exp2(s_t).astype(BF16)

    def weighted_values(kb, slot, i):
        vs = slice((i // 2) * VROWS, (i // 2 + 1) * VROWS)
        return jnp.dot(vt_ref[kb, vs, :], p_ref[slot * units + i], preferred_element_type=F32)

    def unshifted_block(kb, slot, diagonal):
        unshifted_probs(kb, slot, diagonal)
        for i in range(units):
            acc_ref[i] += weighted_values(kb, slot, i)

    def unshifted_pair(t, carry):
        unshifted_probs(2 * t, 0, False)
        unshifted_probs(2 * t + 1, 1, False)
        for i in range(units):
            acc_ref[i] += weighted_values(2 * t, 0, i) + weighted_values(2 * t + 1, 1, i)
        return carry

    bound = (DIFF_HD ** 0.5 * LOG2E * NORM_SLACK) * jnp.max(jnp.abs(qg_ref[...])) * jnp.max(
        jnp.abs(kg_ref[...]))
    small = bound <= MAX_UNSHIFTED_LOG2

    @pl.when(small)
    def _():
        lax.fori_loop(0, qi >> 1, unshifted_pair, 0)

        @pl.when((qi & 1) == 1)
        def _():
            unshifted_block(qi - 1, 0, False)

        unshifted_block(qi, 1, True)

    @pl.when(jnp.logical_not(small))
    def _():
        logits(jnp.int32(0), s0_ref)
        lax.fori_loop(0, qi >> 1, pair_body, 0)

        @pl.when((qi & 1) == 0)
        def _():
            consume(qi, s0_ref, chunk_mask)

        @pl.when((qi & 1) == 1)
        def _():
            logits(qi, s1_ref)
            consume(qi - 1, s0_ref)
            consume(qi, s1_ref, chunk_mask)

    def normalised(i):
        return acc_ref[i, 0:HEAD_W, :] / acc_ref[i, HEAD_W:HEAD_W + 1, :]

    for h in range(DIFF_HEADS):
        hs = slice(h * LANES, (h + 1) * LANES)
        o_t = normalised(2 * h) - lam * normalised(2 * h + 1)
        ms = jnp.mean(o_t * o_t, axis=0, keepdims=True)
        o = ((o_t * lax.rsqrt(ms + EPS)) * (1.0 - lambda_init)).T
        o_ref[:, hs] = (o * _silu(z_ref[:, hs])).astype(BF16)


def _diff_call(q1, q2, z, lq1, lk1, lq2, lk2, qg, kg, k, vt, b, t, lambda_init):
    nq = t // TBLK

    def fm_q():
        return pl.BlockSpec((None, 512, TBLK), lambda bi, i: (bi * nq + i, 0, 0))

    def fs(shape):
        return pl.BlockSpec(shape, lambda bi, i: (0,) * len(shape))

    kernel = functools.partial(_diff_kernel, lambda_init=lambda_init)
    return pl.pallas_call(
        kernel,
        grid=(b, nq),
        in_specs=[fm_q(), fm_q(),
                  pl.BlockSpec((TBLK, 512), lambda bi, i: (bi * nq + i, 0)),
                  fs(lq1.shape), fs(lk1.shape), fs(lq2.shape), fs(lk2.shape),
                  fs(qg.shape), fs(kg.shape),
                  _resident((t, 512), lambda bi, i: (bi, 0)),
                  _resident((nq, 4 * VROWS, TBLK), lambda bi, i: (bi, 0, 0))],
        out_specs=pl.BlockSpec((TBLK, 512), lambda bi, i: (bi * nq + i, 0)),
        out_shape=jax.ShapeDtypeStruct((b * t, BR_W), BF16),
        scratch_shapes=[
            pltpu.VMEM((2 * DIFF_HEADS, VROWS, TBLK), F32),
            pltpu.VMEM((2 * DIFF_HEADS, 1, TBLK), F32),
            pltpu.VMEM((2 * DIFF_HEADS, TBLK, TBLK), F32),
            pltpu.VMEM((2 * DIFF_HEADS, TBLK, TBLK), F32),
            pltpu.VMEM((4 * DIFF_HEADS, TBLK, TBLK), BF16),
        ],
        compiler_params=_cparams(("parallel", "arbitrary")),
        name="diff_mixer",
    )(q1, q2, z, lq1, lk1, lq2, lk2, qg, kg, k, vt)


def _out_kernel(x_ref, g_ref, wg_ref, ya_ref, yb_ref, yc_ref, wbr_ref, wo_ref, o_ref):
    x = x_ref[...]
    hb = _rms_rows(x, g_ref[...]).astype(BF16)
    merged = jnp.zeros(x.shape, F32)
    for i, y_ref in enumerate((ya_ref, yb_ref, yc_ref)):
        gate = _sigmoid(jnp.dot(hb, wg_ref[:, i * D_MODEL:(i + 1) * D_MODEL],
                                preferred_element_type=F32))
        merged = merged + gate * jnp.dot(y_ref[...], wbr_ref[i], preferred_element_type=F32)
    o_ref[...] = x + jnp.dot(merged.astype(BF16), wo_ref[...], preferred_element_type=F32)


def _out_call(x2, g, wg, ya, yb, yc, wbr, wo):
    n = x2.shape[0]
    return pl.pallas_call(
        _out_kernel,
        grid=(n // PROJ_ROWS,),
        in_specs=[_row_spec(D_MODEL), _full_spec(g.shape), _full_spec(wg.shape),
                  _row_spec(BR_W), _row_spec(BR_W), _row_spec(BR_W),
                  _full_spec(wbr.shape), _full_spec(wo.shape)],
        out_specs=_row_spec(D_MODEL),
        out_shape=jax.ShapeDtypeStruct((n, D_MODEL), F32),
        compiler_params=_cparams(("parallel",)),
        name="merge_out",
    )(x2, g, wg, ya, yb, yc, wbr, wo)


def _pad_cols(w, width):
    return jnp.pad(w, ((0, 0), (0, width - w.shape[1])))


def kernel(x, norm_g, w_in, gla_wa2, gla_ba, gla_norm_g, dsa_qn_g, dsa_kn_g, diff_qn_g,
           diff_kn_g, diff_lq1, diff_lk1, diff_lq2, diff_lk2, w_br, w_out):
    b, t, d = x.shape
    depth = w_in.shape[0]
    assert d == D_MODEL and t % max(GLA_ROWS, DSA_KB, TBLK) == 0 and (b * t) % PROJ_ROWS == 0
    n = b * t
    x2 = x.reshape(n, d)

    for l in range(depth):
        w = w_in[l]
        seg = [w[:, OFFS[i]:OFFS[i + 1]] for i in range(len(SIZES))]
        (gq, gk, gv, ga, gz, bq, bk, bv, iq, ik, iw, bz, cq, ck, cv, cz, gate) = seg
        w_gla = jnp.concatenate([gq, gk, gv, _pad_cols(ga, LANES), gz], axis=1).astype(BF16)
        wt_dsa = jnp.concatenate([bq, bv, iq, _pad_cols(iw, 16)], axis=1).T.astype(BF16)
        w_dsa = jnp.concatenate([bk, ik, ik, bz], axis=1).astype(BF16)
        wt_diff = jnp.concatenate([cq, cv], axis=1).T.astype(BF16)
        w_diff = jnp.concatenate([ck, cz], axis=1).astype(BF16)
        w_gate = gate.astype(BF16)
        g = norm_g[l].reshape(1, d)

        g_q, g_k, g_v, g_a, g_z = _proj_call(
            _proj_gla_kernel, "proj_gla", x2, g, [w_gla],
            [_row_out(n, 256, BF16), _row_out(n, 256, BF16), _row_out(n, 512, BF16),
             _row_out(n, 128, BF16), _row_out(n, 512, F32)])
        b_qt, b_k, b_vt, i_qt, i_k, i_wt, b_z = _proj_call(
            _proj_dsa_kernel, "proj_dsa", x2, g,
            [wt_dsa, w_dsa, dsa_qn_g[l].reshape(DSA_HD, 1), dsa_kn_g[l].reshape(1, DSA_HD)],
            [_fm_out(n, 512, BF16), _row_out(n, 512, BF16), _fm_out(n, 4 * VROWS, BF16),
             _fm_out(n, 512, BF16), _row_out(n, LANES, BF16), _fm_out(n, IDX_HEADS, F32),
             _row_out(n, 512, F32)])
        c_q1, c_q2, c_k, c_vt, c_z = _proj_call(
            _proj_diff_kernel, "proj_diff", x2, g,
            [wt_diff, w_diff, diff_qn_g[l].reshape(DIFF_HD, 1),
             jnp.tile(diff_kn_g[l], 2).reshape(1, LANES)],
            [_fm_out(n, 512, BF16), _fm_out(n, 512, BF16), _row_out(n, 512, BF16),
             _fm_out(n, 4 * VROWS, BF16), _row_out(n, 512, F32)])

        wa = jnp.pad(gla_wa2[l], ((0, LANES - GLA_RANK), (0, 0))).astype(BF16)
        y_a = _gla_call(g_q, g_k, g_v, g_a, g_z, wa, gla_ba[l].reshape(1, -1),
                        gla_norm_g[l].reshape(1, GLA_DV), b, t)
        y_b = _dsa_call(b_qt, i_qt, i_wt, b_z, dsa_qn_g[l].reshape(1, DSA_HD),
                        dsa_kn_g[l].reshape(1, DSA_HD), b_k, b_vt, i_k, b, t)
        lambda_init = 0.8 - 0.6 * math.exp(-0.3 * l)
        y_c = _diff_call(c_q1, c_q2, c_z, diff_lq1[l].reshape(1, -1), diff_lk1[l].reshape(1, -1),
                         diff_lq2[l].reshape(1, -1), diff_lk2[l].reshape(1, -1),
                         diff_qn_g[l].reshape(1, -1), diff_kn_g[l].reshape(1, -1), c_k, c_vt,
                         b, t, lambda_init)
        x2 = _out_call(x2, g, w_gate, y_a, y_b, y_c, w_br[l].astype(BF16),
                       w_out[l].astype(BF16))
    return x2.reshape(b, t, d)
```

```python
import functools
import math

import jax
import jax.numpy as jnp
from jax import lax
from jax.experimental import pallas as pl
from jax.experimental.pallas import tpu as pltpu

F32 = jnp.float32
BF16 = jnp.bfloat16
I32 = jnp.int32
I16 = jnp.int16

D_MODEL = 1024
CHUNK = 64
CHUNK_SHIFT = 6
EPS = 1e-6
LANES = 128
SUBLANES = 8

GLA_HEADS, GLA_DK, GLA_DV, GLA_RANK, GLA_TAU = 4, 64, 128, 16, 16.0
DSA_HEADS, DSA_HD, IDX_HEADS, IDX_HD, TOPK_MAX = 4, 128, 8, 64, 256
DIFF_HEADS, DIFF_HD = 4, 64
BR_W, N_BRANCH = 512, 3

SIZES = [
    GLA_HEADS * GLA_DK, GLA_HEADS * GLA_DK, GLA_HEADS * GLA_DV, GLA_RANK, BR_W,
    DSA_HEADS * DSA_HD, DSA_HEADS * DSA_HD, DSA_HEADS * DSA_HD,
    IDX_HEADS * IDX_HD, IDX_HD, IDX_HEADS, BR_W,
    DIFF_HEADS * 2 * DIFF_HD, DIFF_HEADS * 2 * DIFF_HD, DIFF_HEADS * 2 * DIFF_HD, BR_W,
    N_BRANCH * D_MODEL,
]
OFFS = [0]
for _s in SIZES:
    OFFS.append(OFFS[-1] + _s)

INT_MIN = -(2 ** 31)
NEG = -1e30
LOG2E = math.log2(math.e)
NORM_SLACK = 1.02
MAX_UNSHIFTED_LOG2 = 60.0

PROJ_ROWS = 512
GLA_ROWS = 512
TBLK = 256
DSA_KB = 512
ATT_GROUP = 4
SUB = 64
PACK = 16
HEAD_W = 128
VROWS = HEAD_W + PACK
I16_MIN = -(2 ** 15)
VMEM_LIMIT = 56 * 1024 * 1024


def _cparams(sem):
    return pltpu.CompilerParams(dimension_semantics=sem, vmem_limit_bytes=VMEM_LIMIT)


def _resident(shape, index_map):
    return pl.BlockSpec(shape, index_map, pipeline_mode=pl.Buffered(1))


def _sigmoid(x):
    return 1.0 / (1.0 + jnp.exp(-x))


def _silu(x):
    return x * _sigmoid(x)


def _rms_rows(x, g):
    ms = jnp.mean(x * x, axis=-1, keepdims=True)
    return (x * lax.rsqrt(ms + EPS)) * g


def _nt(a, b):
    return lax.dot_general(a, b, (((1,), (1,)), ((), ())), preferred_element_type=F32)


def _fold(x, op):
    return op(x.reshape(x.shape[0] // SUBLANES, SUBLANES, x.shape[1]), axis=0)


def _proj_gla_kernel(x_ref, g_ref, w_ref, q_ref, k_ref, v_ref, a_ref, z_ref):
    hb = _rms_rows(x_ref[...], g_ref[...]).astype(BF16)

    def seg(a, b):
        return jnp.dot(hb, w_ref[:, a:b], preferred_element_type=F32)

    q_ref[...] = seg(0, 256).astype(BF16)
    k_ref[...] = seg(256, 512).astype(BF16)
    v_ref[...] = seg(512, 1024).astype(BF16)
    a_ref[...] = seg(1024, 1152).astype(BF16)
    z_ref[...] = seg(1152, 1664)


def _head_rms(y, g):
    outs = []
    for h in range(4):
        yh = y[:, h * LANES:(h + 1) * LANES]
        ms = jnp.mean(yh * yh, axis=-1, keepdims=True)
        outs.append((yh * lax.rsqrt(ms + EPS)) * g)
    return outs


def _store_values(vt_ref, blk, v_t):
    ones = jnp.ones((PACK, v_t.shape[1]), BF16)
    for h in range(4):
        vt_ref[blk, h * VROWS:h * VROWS + HEAD_W, :] = v_t[h * HEAD_W:(h + 1) * HEAD_W, :]
        vt_ref[blk, h * VROWS + HEAD_W:(h + 1) * VROWS, :] = ones


def _proj_dsa_kernel(x_ref, g_ref, wt_ref, w_ref, qg_ref, kg_ref,
                     qt_ref, k_ref, vt_ref, iqt_ref, ik_ref, iwt_ref, z_ref):
    hb = _rms_rows(x_ref[...], g_ref[...]).astype(BF16)
    nblk = hb.shape[0] // TBLK

    def seg(a, b):
        return jnp.dot(hb, w_ref[:, a:b], preferred_element_type=F32)

    def seg_t(a, b):
        return _nt(wt_ref[a:b, :], hb)

    q_t = seg_t(0, 512)
    for h in range(DSA_HEADS):
        rs = slice(h * DSA_HD, (h + 1) * DSA_HD)
        xh = q_t[rs, :]
        ms = jnp.mean(xh * xh, axis=0, keepdims=True)
        xn = (((xh * lax.rsqrt(ms + EPS)) * qg_ref[...]) * (DSA_HD ** -0.5 * LOG2E)).astype(BF16)
        for blk in range(nblk):
            qt_ref[blk, rs, :] = xn[:, blk * TBLK:(blk + 1) * TBLK]
    v_t = seg_t(512, 1024).astype(BF16)
    iq_t = seg_t(1024, 1536).astype(BF16)
    iw_t = seg_t(1536, 1552)
    for blk in range(nblk):
        cs = slice(blk * TBLK, (blk + 1) * TBLK)
        _store_values(vt_ref, blk, v_t[:, cs])
        iqt_ref[blk] = iq_t[:, cs]
        iwt_ref[blk] = iw_t[0:IDX_HEADS, cs]
    for h, kh in enumerate(_head_rms(seg(0, 512), kg_ref[...])):
        k_ref[:, h * LANES:(h + 1) * LANES] = kh.astype(BF16)
    ik_ref[...] = seg(512, 640).astype(BF16)
    z_ref[...] = seg(640, 1152)


def _half_rms(y, g2):
    lo = lax.broadcasted_iota(I32, (y.shape[0], LANES), 1) < DIFF_HD
    outs = []
    for h in range(4):
        yh = y[:, h * LANES:(h + 1) * LANES]
        sq = yh * yh
        s_lo = jnp.sum(jnp.where(lo, sq, 0.0), axis=-1, keepdims=True)
        s_hi = jnp.sum(jnp.where(lo, 0.0, sq), axis=-1, keepdims=True)
        ms = jnp.where(lo, s_lo, s_hi) * (1.0 / DIFF_HD)
        outs.append((yh * lax.rsqrt(ms + EPS)) * g2)
    return outs


def _proj_diff_kernel(x_ref, g_ref, wt_ref, w_ref, qg_ref, kg_ref,
                      q1_ref, q2_ref, k_ref, vt_ref, z_ref):
    hb = _rms_rows(x_ref[...], g_ref[...]).astype(BF16)
    nblk = hb.shape[0] // TBLK
    q_t = _nt(wt_ref[0:512, :], hb)
    v_t = _nt(wt_ref[512:1024, :], hb)
    zeros = jnp.zeros((DIFF_HD, TBLK), BF16)
    for j in range(2 * DIFF_HEADS):
        rs = slice(j * DIFF_HD, (j + 1) * DIFF_HD)
        xj = q_t[rs, :]
        ms = jnp.mean(xj * xj, axis=0, keepdims=True)
        xn = (((xj * lax.rsqrt(ms + EPS)) * qg_ref[...]) * (DIFF_HD ** -0.5 * LOG2E)).astype(BF16)
        own, other = (q1_ref, q2_ref) if j % 2 == 0 else (q2_ref, q1_ref)
        for blk in range(nblk):
            own[blk, rs, :] = xn[:, blk * TBLK:(blk + 1) * TBLK]
            other[blk, rs, :] = zeros
    for blk in range(nblk):
        _store_values(vt_ref, blk, v_t[:, blk * TBLK:(blk + 1) * TBLK].astype(BF16))
    ks = _half_rms(jnp.dot(hb, w_ref[:, 0:512], preferred_element_type=F32), kg_ref[...])
    for h, kh in enumerate(ks):
        k_ref[:, h * LANES:(h + 1) * LANES] = kh.astype(BF16)
    z_ref[...] = jnp.dot(hb, w_ref[:, 512:1024], preferred_element_type=F32)


def _row_spec(width, rows=PROJ_ROWS):
    return pl.BlockSpec((rows, width), lambda i: (i, 0))


def _full_spec(shape):
    nd = len(shape)
    return pl.BlockSpec(shape, lambda i: (0,) * nd)


def _row_out(n, width, dtype):
    return jax.ShapeDtypeStruct((n, width), dtype), _row_spec(width)


def _fm_out(n, feat, dtype):
    return (jax.ShapeDtypeStruct((n // TBLK, feat, TBLK), dtype),
            pl.BlockSpec((PROJ_ROWS // TBLK, feat, TBLK), lambda i: (i, 0, 0)))


def _proj_call(kernel, name, x2, g, consts, outs):
    n = x2.shape[0]
    in_specs = [_row_spec(D_MODEL), _full_spec(g.shape)] + [_full_spec(e.shape) for e in consts]
    return pl.pallas_call(
        kernel,
        grid=(n // PROJ_ROWS,),
        in_specs=in_specs,
        out_specs=[spec for _, spec in outs],
        out_shape=[shape for shape, _ in outs],
        compiler_params=_cparams(("parallel",)),
        name=name,
    )(x2, g, *consts)


def _gla_kernel(q_ref, k_ref, v_ref, a_ref, z_ref, wa_ref, ba_ref, ng_ref, o_ref, st_ref):
    nbatch, rows = q_ref.shape[0], q_ref.shape[1]
    nchunk = rows // CHUNK

    @pl.when(pl.program_id(0) == 0)
    def _():
        st_ref[...] = jnp.zeros_like(st_ref)

    pos = lax.broadcasted_iota(I32, (rows, 1), 0) & (CHUNK - 1)
    lo = lax.broadcasted_iota(I32, (CHUNK, LANES), 1) < GLA_DK
    cums = []
    for bb in range(nbatch):
        pre = jnp.dot(a_ref[bb], wa_ref[...], preferred_element_type=F32) + ba_ref[...]
        cum = -(jnp.maximum(-pre, 0.0) + jnp.log1p(jnp.exp(-jnp.abs(pre)))) / GLA_TAU
        s = 1
        while s < CHUNK:
            cum = cum + jnp.where(pos >= s, pltpu.roll(cum, s, axis=0), 0.0)
            s *= 2
        cums.append(cum)

    for c in range(nchunk):
        r0 = c * CHUNK
        for bb in range(nbatch):
            cum = cums[bb]
            total = cum[r0 + CHUNK - 1:r0 + CHUNK, :]
            decay_to_end = jnp.exp(total - cum[r0:r0 + CHUNK, :])
            k_dec = (k_ref[bb, r0:r0 + CHUNK, :].astype(F32) * decay_to_end).astype(BF16)
            a_tot = jnp.exp(total)
            q_c = q_ref[bb, r0:r0 + CHUNK, :].astype(F32) * (GLA_DK ** -0.5)
            for h in range(GLA_HEADS):
                j = h // 2
                pair = slice(j * LANES, (j + 1) * LANES)
                hs = slice(h * GLA_DV, (h + 1) * GLA_DV)
                u_t = lax.dot_general(v_ref[bb, r0:r0 + CHUNK, hs], k_dec[:, pair],
                                      (((0,), (0,)), ((), ())),
                                      preferred_element_type=F32)
                st = st_ref[bb * GLA_HEADS + h] * a_tot[:, pair] + u_t
                st_ref[bb * GLA_HEADS + h] = st
                keep = lo if h % 2 == 0 else jnp.logical_not(lo)
                q_h = jnp.where(keep, q_c[:, pair], 0.0).astype(BF16)
                o = _nt(q_h, st.astype(BF16))
                ms = jnp.mean(o * o, axis=-1, keepdims=True)
                o = (o * lax.rsqrt(ms + EPS)) * ng_ref[...]
                zh = z_ref[bb, r0:r0 + CHUNK, hs]
                o_ref[bb, r0:r0 + CHUNK, hs] = (o * _silu(zh)).astype(BF16)


def _gla_call(q, k, v, a, z, wa, ba, ng, b, t):
    nb = t // GLA_ROWS

    def rs(width):
        return pl.BlockSpec((b, GLA_ROWS, width), lambda i: (0, i, 0))

    def fs(shape):
        return pl.BlockSpec(shape, lambda i: (0,) * len(shape))

    def by_batch(x):
        return x.reshape(b, t, x.shape[-1])

    out = pl.pallas_call(
        _gla_kernel,
        grid=(nb,),
        in_specs=[rs(256), rs(256), rs(512), rs(128), rs(512),
                  fs(wa.shape), fs(ba.shape), fs(ng.shape)],
        out_specs=rs(512),
        out_shape=jax.ShapeDtypeStruct((b, t, BR_W), BF16),
        scratch_shapes=[pltpu.VMEM((b * GLA_HEADS, GLA_DV, LANES), F32)],
        compiler_params=_cparams(("arbitrary",)),
        name="gla_mixer",
    )(by_batch(q), by_batch(k), by_batch(v), by_batch(a), by_batch(z), wa, ba, ng)
    return out.reshape(b * t, BR_W)


def _softmax_unit(s_ref, p_ref, m_ref, i, allowed=None):
    nk = s_ref.shape[1]

    def piece(r):
        s = s_ref[i, r:r + SUB, :]
        return s if allowed is None else jnp.where(allowed(r), s, NEG)

    mx = None
    for r in range(0, nk, SUB):
        part = _fold(piece(r), jnp.max)
        mx = part if mx is None else jnp.maximum(mx, part)
    m_old = m_ref[i]
    m_new = jnp.maximum(m_old, jnp.max(mx, axis=0, keepdims=True))
    for r in range(0, nk, SUB):
        p_ref[i, r:r + SUB, :] = jnp.exp2((piece(r) - m_new).astype(BF16))
    m_ref[i] = m_new
    return jnp.exp2(m_old - m_new)


def _dsa_kernel(qt_ref, iqt_ref, iwt_ref, z_ref, qg_ref, kg_ref, k_ref, vt_ref, ik_ref, o_ref,
                sc_ref, hi_ref, lo_ref, tau_ref, nge_ref, iqm_ref, d_ref, acc_ref, m_ref,
                s0_ref, s1_ref, p_ref, *, topk, idx_bits):
    nq = qt_ref.shape[1]
    kb_sz = sc_ref.shape[1]
    nhalf = kb_sz // TBLK
    qi = pl.program_id(1)
    q0 = qi * nq
    nkb = (q0 + nq + kb_sz - 1) >> (kb_sz.bit_length() - 1)

    q_end = q0 + ((lax.broadcasted_iota(I32, (1, nq), 1) >> CHUNK_SHIFT) + 1) * CHUNK
    sub_iota = lax.broadcasted_iota(I32, (SUB, nq), 0)

    upper = lax.broadcasted_iota(I32, (LANES, nq), 0) >= IDX_HD
    for h in range(IDX_HEADS):
        pair = iqt_ref[(h // 2) * LANES:(h // 2 + 1) * LANES, :].astype(F32)
        keep = upper if h % 2 else jnp.logical_not(upper)
        iqm_ref[h] = jnp.where(keep, pair, 0.0).astype(BF16)
    wc = iwt_ref[...] * (IDX_HEADS ** -0.5 * IDX_HD ** -0.5)

    assert nhalf == 2
    n_sub = nkb * nhalf

    def head_dots(j, half):
        ikb = ik_ref[pl.ds(pl.multiple_of(j * TBLK, TBLK), TBLK), :]
        for h in range(IDX_HEADS):
            d_ref[half * IDX_HEADS + h] = jnp.dot(ikb, iqm_ref[h], preferred_element_type=F32)

    def combine(kb, half):
        d0 = half * IDX_HEADS
        for r in range(0, TBLK, SUB):
            s = jnp.zeros((SUB, nq), F32)
            for h in range(IDX_HEADS):
                s = s + jnp.maximum(d_ref[d0 + h, r:r + SUB, :], 0.0) * wc[h:h + 1, :]
            store_scores(kb, half * TBLK + r, s, None)

    def score_pair(t, carry):
        head_dots(4 * t + 1, 1)
        combine(2 * t, 0)
        head_dots(4 * t + 2, 0)
        combine(2 * t, 1)
        head_dots(4 * t + 3, 1)
        combine(2 * t + 1, 0)
        head_dots(jnp.minimum(4 * t + 4, n_sub - 1), 0)
        combine(2 * t + 1, 1)
        return carry

    def store_scores(kb, r, s, visible):
        rs = slice(r, r + SUB)
        bits = lax.bitcast_convert_type(s, I32)
        sign = bits >> 31
        key = ((bits & 0x7FFFFFFF) ^ sign) - sign
        if visible is not None:
            s = jnp.where(visible, s, -jnp.inf)
            key = jnp.where(visible, key, INT_MIN)
        sc_ref[kb, rs, :] = s
        hi_ref[kb, rs, :] = (key >> 16).astype(I16)
        lo_ref[kb, rs, :] = ((key & 0xFFFF) + I16_MIN).astype(I16)

    head_dots(jnp.int32(0), 0)
    lax.fori_loop(0, nkb >> 1, score_pair, 0)

    @pl.when((nkb & 1) == 1)
    def _():
        head_dots(n_sub - 1, 1)
        combine(nkb - 1, 0)
        combine(nkb - 1, 1)

    for r in range(0, kb_sz, SUB):
        pos = (nkb - 1) * kb_sz + r + sub_iota
        store_scores(nkb - 1, r, sc_ref[nkb - 1, r:r + SUB, :], pos < q_end)


    def count(*preds):
        def body(kb, parts):
            parts = list(parts)
            for r in range(0, kb_sz, SUB):
                sv = sc_ref[kb, r:r + SUB, :]
                for n, pred in enumerate(preds):
                    hit = pred(sv, kb * kb_sz + r)
                    parts[n] = parts[n] + _fold(jnp.where(hit, 1.0, 0.0), jnp.sum)
            return tuple(parts)
        parts = lax.fori_loop(0, nkb, body, (jnp.zeros((SUBLANES, nq), F32),) * len(preds))
        sums = [jnp.sum(part, axis=0, keepdims=True) for part in parts]
        return sums[0] if len(sums) == 1 else sums

    n_acc = 4
    piece = n_acc * PACK * 2
    one16, zero16 = jnp.ones((), I16), jnp.zeros((), I16)

    def count16(plane_ref, pred):
        def block(kb, accs):
            accs = list(accs)
            for r in range(0, kb_sz, piece):
                ones = jnp.where(pred(plane_ref[kb, r:r + piece, :]), one16, zero16)
                for j in range(piece // PACK):
                    accs[j % n_acc] = accs[j % n_acc] + ones[j * PACK:(j + 1) * PACK, :]
            return tuple(accs)

        def pair(t, accs):
            return block(2 * t + 1, block(2 * t, accs))

        accs = lax.fori_loop(0, nkb >> 1, pair, (jnp.zeros((PACK, nq), I16),) * n_acc)
        accs = lax.fori_loop(nkb - (nkb & 1), nkb, block, accs)
        tot = (accs[0] + accs[1]) + (accs[2] + accs[3])
        return jnp.sum(tot.astype(I32).astype(F32), axis=0, keepdims=True)

    def search16(plane_ref, want, c_start):
        def bit_body(i, carry):
            res, cres = carry
            cand = res | lax.shift_left(jnp.int32(1), 15 - i)
            cand16 = (cand + I16_MIN).astype(I16)
            cnt = count16(plane_ref, lambda kv: kv >= cand16)
            take = cnt >= want
            return jnp.where(take, cand, res), jnp.where(take, cnt, cres)
        return lax.fori_loop(0, 16, bit_body, (jnp.zeros((1, nq), I32), c_start))

    zero_f = jnp.zeros((1, nq), F32)
    hi_u, c_ge_hi = search16(hi_ref, jnp.full((1, nq), float(topk), F32), zero_f)
    hi16 = (hi_u + I16_MIN).astype(I16)
    c_gt_hi = count16(hi_ref, lambda kv: kv > hi16)
    low_min = jnp.full((), I16_MIN, I16)

    def bucket_body(kb, carry):
        for r in range(0, kb_sz, piece):
            rs = slice(r, r + piece)
            lo_ref[kb, rs, :] = jnp.where(hi_ref[kb, rs, :] == hi16, lo_ref[kb, rs, :], low_min)
        return carry

    lax.fori_loop(0, nkb, bucket_body, 0)
    lo_u, _ = search16(lo_ref, topk - c_gt_hi, c_ge_hi - c_gt_hi)
    thr_key = ((hi_u + I16_MIN) << 16) | lo_u

    def key_to_score(key):
        bits = jnp.where(key < 0, (-key) | INT_MIN, key)
        return lax.bitcast_convert_type(bits, F32)

    few = q_end <= topk
    lowest = float(jnp.finfo(F32).min)
    tau0 = jnp.where(few, lowest, key_to_score(thr_key))

    n_ge0 = count(lambda sv, _: sv >= tau0)
    confirmed = jnp.logical_or(few, n_ge0 == topk)
    tau_ref[...] = tau0
    nge_ref[...] = n_ge0

    @pl.when(jnp.min(jnp.where(confirmed, 1.0, 0.0)) < 1.0)
    def _():
        def bit_body(i, carry):
            res, cres = carry
            cand = res | lax.shift_left(jnp.int32(1), 31 - i)
            cand_f = key_to_score(cand ^ INT_MIN)
            cnt = count(lambda sv, _: sv >= cand_f)
            take = cnt >= topk
            return jnp.where(take, cand, res), jnp.where(take, cnt, cres)

        res, cres = lax.fori_loop(0, 32, bit_body,
                                  (jnp.zeros((1, nq), I32), jnp.zeros((1, nq), F32)))
        tau_ref[...] = jnp.where(few, lowest, key_to_score(res ^ INT_MIN))
        nge_ref[...] = cres

    tau = tau_ref[...]

    excess = jnp.where(jnp.logical_and(nge_ref[...] > topk, jnp.logical_not(few)), 1.0, 0.0)

    @pl.when(jnp.max(excess) > 0.0)
    def _():
        need = topk - count(lambda sv, _: sv > tau)

        def idx_body(i, p):
            cand = p | lax.shift_left(jnp.int32(1), idx_bits - 1 - i)
            below = count(lambda sv, r0: jnp.logical_and(sv == tau, r0 + sub_iota < cand))
            return jnp.where(below <= need - 1.0, cand, p)

        last = lax.fori_loop(0, idx_bits, idx_body, jnp.zeros((1, nq), I32))
        excess_b = excess > 0.0

        def demote_body(kb, carry):
            for r in range(0, kb_sz, SUB):
                sv = sc_ref[kb, r:r + SUB, :]
                drop = jnp.logical_and(jnp.logical_and(sv == tau, excess_b),
                                       kb * kb_sz + r + sub_iota > last)
                sc_ref[kb, r:r + SUB, :] = jnp.where(drop, -jnp.inf, sv)
            return carry

        lax.fori_loop(0, nkb, demote_body, 0)

    acc_ref[...] = jnp.zeros_like(acc_ref)
    m_ref[...] = jnp.full_like(m_ref, NEG)

    nblk = (q0 + nq) >> (TBLK.bit_length() - 1)
    half_shift = nhalf.bit_length() - 1

    def logits(j, s_ref):
        kstart = pl.multiple_of(j * TBLK, TBLK)
        krows = pl.ds(pl.multiple_of((j & (nhalf - 1)) * TBLK, TBLK), TBLK)
        for h in range(DSA_HEADS):
            hs = slice(h * DSA_HD, (h + 1) * DSA_HD)
            s_t = jnp.dot(k_ref[pl.ds(kstart, TBLK), hs], qt_ref[hs, :],
                          preferred_element_type=F32)
            sel = sc_ref[j >> half_shift, krows, :] >= tau
            s_ref[h] = jnp.where(sel, s_t, NEG)

    def consume(j, s_ref):
        for h in range(DSA_HEADS):
            alpha = _softmax_unit(s_ref, p_ref, m_ref, h)
            acc_ref[h] = alpha * acc_ref[h] + jnp.dot(
                vt_ref[j, h * VROWS:(h + 1) * VROWS, :], p_ref[h],
                preferred_element_type=F32)

    def pair_body(t, carry):
        logits(2 * t + 1, s1_ref)
        consume(2 * t, s0_ref)
        logits(jnp.minimum(2 * t + 2, nblk - 1), s0_ref)
        consume(2 * t + 1, s1_ref)
        return carry

    def unshifted_probs(j, slot):
        kstart = pl.multiple_of(j * TBLK, TBLK)
        krows = pl.ds(pl.multiple_of((j & (nhalf - 1)) * TBLK, TBLK), TBLK)
        for h in range(DSA_HEADS):
            hs = slice(h * DSA_HD, (h + 1) * DSA_HD)
            s_t = jnp.dot(k_ref[pl.ds(kstart, TBLK), hs], qt_ref[hs, :],
                          preferred_element_type=F32)
            sel = sc_ref[j >> half_shift, krows, :] >= tau
            p_ref[slot * DSA_HEADS + h] = jnp.exp2(jnp.where(sel, s_t, NEG)).astype(BF16)

    def weighted_values(j, slot, h):
        return jnp.dot(vt_ref[j, h * VROWS:(h + 1) * VROWS, :], p_ref[slot * DSA_HEADS + h],
                       preferred_element_type=F32)

    def unshifted_blocks(j0, n):
        for s in range(n):
            unshifted_probs(j0 + s, s)
        for h in range(DSA_HEADS):
            total = weighted_values(j0, 0, h)
            for s in range(1, n):
                total = total + weighted_values(j0 + s, s, h)
            acc_ref[h] += total

    def unshifted_group(t, carry):
        unshifted_blocks(ATT_GROUP * t, ATT_GROUP)
        return carry

    bound = (DSA_HD ** 0.5 * LOG2E * NORM_SLACK) * jnp.max(jnp.abs(qg_ref[...])) * jnp.max(
        jnp.abs(kg_ref[...]))
    small = bound <= MAX_UNSHIFTED_LOG2

    @pl.when(small)
    def _():
        ngroup = nblk >> (ATT_GROUP.bit_length() - 1)
        lax.fori_loop(0, ngroup, unshifted_group, 0)
        done = ngroup * ATT_GROUP

        @pl.when((nblk & 2) != 0)
        def _():
            unshifted_blocks(done, 2)

        @pl.when((nblk & 1) != 0)
        def _():
            unshifted_blocks(nblk - 1, 1)

    @pl.when(jnp.logical_not(small))
    def _():
        logits(jnp.int32(0), s0_ref)
        lax.fori_loop(0, nblk >> 1, pair_body, 0)

        @pl.when((nblk & 1) == 1)
        def _():
            consume(nblk - 1, s0_ref)

    for h in range(DSA_HEADS):
        hs = slice(h * DSA_HD, (h + 1) * DSA_HD)
        o = (acc_ref[h, 0:HEAD_W, :] / acc_ref[h, HEAD_W:HEAD_W + 1, :]).T
        o_ref[:, hs] = (o * _silu(z_ref[:, hs])).astype(BF16)


def _dsa_call(qt, iqt, iwt, z, qg, kg, k, vt, ik, b, t):
    nq = t // TBLK

    def fs(shape):
        return pl.BlockSpec(shape, lambda bi, i: (0,) * len(shape))

    topk = min(TOPK_MAX, t // 4)
    idx_bits = max(1, (t - 1).bit_length())

    def fm_q(feat):
        return pl.BlockSpec((None, feat, TBLK), lambda bi, i: (bi * nq + i, 0, 0))

    def rows_q(width):
        return pl.BlockSpec((TBLK, width), lambda bi, i: (bi * nq + i, 0))

    def rows_b(width):
        return _resident((t, width), lambda bi, i: (bi, 0))

    kernel = functools.partial(_dsa_kernel, topk=topk, idx_bits=idx_bits)
    return pl.pallas_call(
        kernel,
        grid=(b, nq),
        in_specs=[fm_q(512), fm_q(512), fm_q(IDX_HEADS), rows_q(512), fs(qg.shape), fs(kg.shape),
                  rows_b(512),
                  _resident((nq, 4 * VROWS, TBLK), lambda bi, i: (bi, 0, 0)), rows_b(LANES)],
        out_specs=rows_q(512),
        out_shape=jax.ShapeDtypeStruct((b * t, BR_W), BF16),
        scratch_shapes=[
            pltpu.VMEM((t // DSA_KB, DSA_KB, TBLK), F32),
            pltpu.VMEM((t // DSA_KB, DSA_KB, TBLK), I16),
            pltpu.VMEM((t // DSA_KB, DSA_KB, TBLK), I16),
            pltpu.VMEM((1, TBLK), F32),
            pltpu.VMEM((1, TBLK), F32),
            pltpu.VMEM((IDX_HEADS, LANES, TBLK), BF16),
            pltpu.VMEM((DSA_KB // TBLK * IDX_HEADS, TBLK, TBLK), F32),
            pltpu.VMEM((DSA_HEADS, VROWS, TBLK), F32),
            pltpu.VMEM((DSA_HEADS, 1, TBLK), F32),
            pltpu.VMEM((DSA_HEADS, TBLK, TBLK), F32),
            pltpu.VMEM((DSA_HEADS, TBLK, TBLK), F32),
            pltpu.VMEM((ATT_GROUP * DSA_HEADS, TBLK, TBLK), BF16),
        ],
        compiler_params=_cparams(("parallel", "arbitrary")),
        name="dsa_mixer",
    )(qt, iqt, iwt, z, qg, kg, k, vt, ik)


def _diff_kernel(q1_ref, q2_ref, z_ref, lq1_ref, lk1_ref, lq2_ref, lk2_ref, qg_ref, kg_ref,
                 k_ref, vt_ref, o_ref, acc_ref, m_ref, s0_ref, s1_ref, p_ref, *, lambda_init):
    nq = q1_ref.shape[1]
    qi = pl.program_id(1)
    lam = (jnp.exp(jnp.sum(lq1_ref[...] * lk1_ref[...], axis=-1, keepdims=True))
           - jnp.exp(jnp.sum(lq2_ref[...] * lk2_ref[...], axis=-1, keepdims=True))
           + lambda_init)

    acc_ref[...] = jnp.zeros_like(acc_ref)
    m_ref[...] = jnp.full_like(m_ref, NEG)

    q_end = ((lax.broadcasted_iota(I32, (1, nq), 1) >> CHUNK_SHIFT) + 1) * CHUNK
    sub_iota = lax.broadcasted_iota(I32, (SUB, nq), 0)

    def logits(kb, s_ref):
        rows = pl.ds(pl.multiple_of(kb * nq, nq), nq)
        for h in range(DIFF_HEADS):
            hs = slice(h * LANES, (h + 1) * LANES)
            k_h = k_ref[rows, hs]
            for c, q_ref in enumerate((q1_ref, q2_ref)):
                s_ref[2 * h + c] = jnp.dot(k_h, q_ref[hs, :], preferred_element_type=F32)

    def consume(kb, s_ref, allowed=None):
        for i in range(2 * DIFF_HEADS):
            vs = slice((i // 2) * VROWS, (i // 2 + 1) * VROWS)
            alpha = _softmax_unit(s_ref, p_ref, m_ref, i, allowed)
            acc_ref[i] = alpha * acc_ref[i] + jnp.dot(vt_ref[kb, vs, :], p_ref[i],
                                                      preferred_element_type=F32)

    def pair_body(t, carry):
        logits(2 * t + 1, s1_ref)
        consume(2 * t, s0_ref)
        logits(2 * t + 2, s0_ref)
        consume(2 * t + 1, s1_ref)
        return carry

    def chunk_mask(r):
        return r + sub_iota < q_end

    units = 2 * DIFF_HEADS

    def unshifted_probs(kb, slot, diagonal):
        rows = pl.ds(pl.multiple_of(kb * nq, nq), nq)
        for h in range(DIFF_HEADS):
            hs = slice(h * LANES, (h + 1) * LANES)
            k_h = k_ref[rows, hs]
            for c, q_ref in enumerate((q1_ref, q2_ref)):
                s_t = jnp.dot(k_h, q_ref[hs, :], preferred_element_type=F32)
                if diagonal:
                    s_t = jnp.where(lax.broadcasted_iota(I32, (nq, nq), 0) < q_end, s_t, NEG)
                p_ref[slot * units + 2 * h + c] = jnp.exp2(s_t).astype(BF16)

    def weighted_values(kb, slot, i):
        vs = slice((i // 2) * VROWS, (i // 2 + 1) * VROWS)
        return jnp.dot(vt_ref[kb, vs, :], p_ref[slot * units + i], preferred_element_type=F32)

    def unshifted_blocks(kb0, n, diagonal_last):
        for s in range(n):
            unshifted_probs(kb0 + s, s, diagonal_last and s == n - 1)
        for i in range(units):
            total = weighted_values(kb0, 0, i)
            for s in range(1, n):
                total = total + weighted_values(kb0 + s, s, i)
            acc_ref[i] += total

    def unshifted_group(t, carry):
        unshifted_blocks(ATT_GROUP * t, ATT_GROUP, False)
        return carry

    bound = (DIFF_HD ** 0.5 * LOG2E * NORM_SLACK) * jnp.max(jnp.abs(qg_ref[...])) * jnp.max(
        jnp.abs(kg_ref[...]))
    small = bound <= MAX_UNSHIFTED_LOG2

    @pl.when(small)
    def _():
        ngroup = qi >> (ATT_GROUP.bit_length() - 1)
        lax.fori_loop(0, ngroup, unshifted_group, 0)
        done = ngroup * ATT_GROUP

        @pl.when((qi & 2) != 0)
        def _():
            unshifted_blocks(done, 2, False)

        @pl.when((qi & 1) != 0)
        def _():
            unshifted_blocks(qi - 1, 2, True)

        @pl.when((qi & 1) == 0)
        def _():
            unshifted_blocks(qi, 1, True)

    @pl.when(jnp.logical_not(small))
    def _():
        logits(jnp.int32(0), s0_ref)
        lax.fori_loop(0, qi >> 1, pair_body, 0)

        @pl.when((qi & 1) == 0)
        def _():
            consume(qi, s0_ref, chunk_mask)

        @pl.when((qi & 1) == 1)
        def _():
            logits(qi, s1_ref)
            consume(qi - 1, s0_ref)
            consume(qi, s1_ref, chunk_mask)

    def normalised(i):
        return acc_ref[i, 0:HEAD_W, :] / acc_ref[i, HEAD_W:HEAD_W + 1, :]

    for h in range(DIFF_HEADS):
        hs = slice(h * LANES, (h + 1) * LANES)
        o_t = normalised(2 * h) - lam * normalised(2 * h + 1)
        ms = jnp.mean(o_t * o_t, axis=0, keepdims=True)
        o = ((o_t * lax.rsqrt(ms + EPS)) * (1.0 - lambda_init)).T
        o_ref[:, hs] = (o * _silu(z_ref[:, hs])).astype(BF16)


def _diff_call(q1, q2, z, lq1, lk1, lq2, lk2, qg, kg, k, vt, b, t, lambda_init):
    nq = t // TBLK

    def fm_q():
        return pl.BlockSpec((None, 512, TBLK), lambda bi, i: (bi * nq + i, 0, 0))

    def fs(shape):
        return pl.BlockSpec(shape, lambda bi, i: (0,) * len(shape))

    kernel = functools.partial(_diff_kernel, lambda_init=lambda_init)
    return pl.pallas_call(
        kernel,
        grid=(b, nq),
        in_specs=[fm_q(), fm_q(),
                  pl.BlockSpec((TBLK, 512), lambda bi, i: (bi * nq + i, 0)),
                  fs(lq1.shape), fs(lk1.shape), fs(lq2.shape), fs(lk2.shape),
                  fs(qg.shape), fs(kg.shape),
                  _resident((t, 512), lambda bi, i: (bi, 0)),
                  _resident((nq, 4 * VROWS, TBLK), lambda bi, i: (bi, 0, 0))],
        out_specs=pl.BlockSpec((TBLK, 512), lambda bi, i: (bi * nq + i, 0)),
        out_shape=jax.ShapeDtypeStruct((b * t, BR_W), BF16),
        scratch_shapes=[
            pltpu.VMEM((2 * DIFF_HEADS, VROWS, TBLK), F32),
            pltpu.VMEM((2 * DIFF_HEADS, 1, TBLK), F32),
            pltpu.VMEM((2 * DIFF_HEADS, TBLK, TBLK), F32),
            pltpu.VMEM((2 * DIFF_HEADS, TBLK, TBLK), F32),
            pltpu.VMEM((ATT_GROUP * 2 * DIFF_HEADS, TBLK, TBLK), BF16),
        ],
        compiler_params=_cparams(("parallel", "arbitrary")),
        name="diff_mixer",
    )(q1, q2, z, lq1, lk1, lq2, lk2, qg, kg, k, vt)


def _out_kernel(x_ref, g_ref, wg_ref, ya_ref, yb_ref, yc_ref, wbr_ref, wo_ref, o_ref):
    x = x_ref[...]
    hb = _rms_rows(x, g_ref[...]).astype(BF16)
    merged = jnp.zeros(x.shape, F32)
    for i, y_ref in enumerate((ya_ref, yb_ref, yc_ref)):
        gate = _sigmoid(jnp.dot(hb, wg_ref[:, i * D_MODEL:(i + 1) * D_MODEL],
                                preferred_element_type=F32))
        merged = merged + gate * jnp.dot(y_ref[...], wbr_ref[i], preferred_element_type=F32)
    o_ref[...] = x + jnp.dot(merged.astype(BF16), wo_ref[...], preferred_element_type=F32)


def _out_call(x2, g, wg, ya, yb, yc, wbr, wo):
    n = x2.shape[0]
    return pl.pallas_call(
        _out_kernel,
        grid=(n // PROJ_ROWS,),
        in_specs=[_row_spec(D_MODEL), _full_spec(g.shape), _full_spec(wg.shape),
                  _row_spec(BR_W), _row_spec(BR_W), _row_spec(BR_W),
                  _full_spec(wbr.shape), _full_spec(wo.shape)],
        out_specs=_row_spec(D_MODEL),
        out_shape=jax.ShapeDtypeStruct((n, D_MODEL), F32),
        compiler_params=_cparams(("parallel",)),
        name="merge_out",
    )(x2, g, wg, ya, yb, yc, wbr, wo)


def _pad_cols(w, width):
    return jnp.pad(w, ((0, 0), (0, width - w.shape[1])))


def kernel(x, norm_g, w_in, gla_wa2, gla_ba, gla_norm_g, dsa_qn_g, dsa_kn_g, diff_qn_g,
           diff_kn_g, diff_lq1, diff_lk1, diff_lq2, diff_lk2, w_br, w_out):
    b, t, d = x.shape
    depth = w_in.shape[0]
    assert d == D_MODEL and t % max(GLA_ROWS, DSA_KB, TBLK) == 0 and (b * t) % PROJ_ROWS == 0
    n = b * t
    x2 = x.reshape(n, d)

    for l in range(depth):
        w = w_in[l]
        seg = [w[:, OFFS[i]:OFFS[i + 1]] for i in range(len(SIZES))]
        (gq, gk, gv, ga, gz, bq, bk, bv, iq, ik, iw, bz, cq, ck, cv, cz, gate) = seg
        w_gla = jnp.concatenate([gq, gk, gv, _pad_cols(ga, LANES), gz], axis=1).astype(BF16)
        wt_dsa = jnp.concatenate([bq, bv, iq, _pad_cols(iw, 16)], axis=1).T.astype(BF16)
        w_dsa = jnp.concatenate([bk, ik, ik, bz], axis=1).astype(BF16)
        wt_diff = jnp.concatenate([cq, cv], axis=1).T.astype(BF16)
        w_diff = jnp.concatenate([ck, cz], axis=1).astype(BF16)
        w_gate = gate.astype(BF16)
        g = norm_g[l].reshape(1, d)

        g_q, g_k, g_v, g_a, g_z = _proj_call(
            _proj_gla_kernel, "proj_gla", x2, g, [w_gla],
            [_row_out(n, 256, BF16), _row_out(n, 256, BF16), _row_out(n, 512, BF16),
             _row_out(n, 128, BF16), _row_out(n, 512, F32)])
        b_qt, b_k, b_vt, i_qt, i_k, i_wt, b_z = _proj_call(
            _proj_dsa_kernel, "proj_dsa", x2, g,
            [wt_dsa, w_dsa, dsa_qn_g[l].reshape(DSA_HD, 1), dsa_kn_g[l].reshape(1, DSA_HD)],
            [_fm_out(n, 512, BF16), _row_out(n, 512, BF16), _fm_out(n, 4 * VROWS, BF16),
             _fm_out(n, 512, BF16), _row_out(n, LANES, BF16), _fm_out(n, IDX_HEADS, F32),
             _row_out(n, 512, F32)])
        c_q1, c_q2, c_k, c_vt, c_z = _proj_call(
            _proj_diff_kernel, "proj_diff", x2, g,
            [wt_diff, w_diff, diff_qn_g[l].reshape(DIFF_HD, 1),
             jnp.tile(diff_kn_g[l], 2).reshape(1, LANES)],
            [_fm_out(n, 512, BF16), _fm_out(n, 512, BF16), _row_out(n, 512, BF16),
             _fm_out(n, 4 * VROWS, BF16), _row_out(n, 512, F32)])

        wa = jnp.pad(gla_wa2[l], ((0, LANES - GLA_RANK), (0, 0))).astype(BF16)
        y_a = _gla_call(g_q, g_k, g_v, g_a, g_z, wa, gla_ba[l].reshape(1, -1),
                        gla_norm_g[l].reshape(1, GLA_DV), b, t)
        y_b = _dsa_call(b_qt, i_qt, i_wt, b_z, dsa_qn_g[l].reshape(1, DSA_HD),
                        dsa_kn_g[l].reshape(1, DSA_HD), b_k, b_vt, i_k, b, t)
        lambda_init = 0.8 - 0.6 * math.exp(-0.3 * l)
        y_c = _diff_call(c_q1, c_q2, c_z, diff_lq1[l].reshape(1, -1), diff_lk1[l].reshape(1, -1),
                         diff_lq2[l].reshape(1, -1), diff_lk2[l].reshape(1, -1),
                         diff_qn_g[l].reshape(1, -1), diff_kn_g[l].reshape(1, -1), c_k, c_vt,
                         b, t, lambda_init)
        x2 = _out_call(x2, g, w_gate, y_a, y_b, y_c, w_br[l].astype(BF16),
                       w_out[l].astype(BF16))
    return x2.reshape(b, t, d)
```

```python
import functools
import math

import jax
import jax.numpy as jnp
from jax import lax
from jax.experimental import pallas as pl
from jax.experimental.pallas import tpu as pltpu

F32 = jnp.float32
BF16 = jnp.bfloat16
I32 = jnp.int32
I16 = jnp.int16

D_MODEL = 1024
CHUNK = 64
CHUNK_SHIFT = 6
EPS = 1e-6
LANES = 128
SUBLANES = 8

GLA_HEADS, GLA_DK, GLA_DV, GLA_RANK, GLA_TAU = 4, 64, 128, 16, 16.0
DSA_HEADS, DSA_HD, IDX_HEADS, IDX_HD, TOPK_MAX = 4, 128, 8, 64, 256
DIFF_HEADS, DIFF_HD = 4, 64
BR_W, N_BRANCH = 512, 3

SIZES = [
    GLA_HEADS * GLA_DK, GLA_HEADS * GLA_DK, GLA_HEADS * GLA_DV, GLA_RANK, BR_W,
    DSA_HEADS * DSA_HD, DSA_HEADS * DSA_HD, DSA_HEADS * DSA_HD,
    IDX_HEADS * IDX_HD, IDX_HD, IDX_HEADS, BR_W,
    DIFF_HEADS * 2 * DIFF_HD, DIFF_HEADS * 2 * DIFF_HD, DIFF_HEADS * 2 * DIFF_HD, BR_W,
    N_BRANCH * D_MODEL,
]
OFFS = [0]
for _s in SIZES:
    OFFS.append(OFFS[-1] + _s)

INT_MIN = -(2 ** 31)
NEG = -1e30
LOG2E = math.log2(math.e)
NORM_SLACK = 1.02
MAX_UNSHIFTED_LOG2 = 60.0

PROJ_ROWS = 512
GLA_ROWS = 512
TBLK = 256
DSA_KB = 512
ATT_GROUP = 4
SUB = 64
PACK = 16
HEAD_W = 128
VROWS = HEAD_W + PACK
I16_MIN = -(2 ** 15)
VMEM_LIMIT = 56 * 1024 * 1024


def _cparams(sem):
    return pltpu.CompilerParams(dimension_semantics=sem, vmem_limit_bytes=VMEM_LIMIT)


def _resident(shape, index_map):
    return pl.BlockSpec(shape, index_map, pipeline_mode=pl.Buffered(1))


def _sigmoid(x):
    return 1.0 / (1.0 + jnp.exp(-x))


def _silu(x):
    return x * _sigmoid(x)


def _rms_rows(x, g):
    ms = jnp.mean(x * x, axis=-1, keepdims=True)
    return (x * lax.rsqrt(ms + EPS)) * g


def _nt(a, b):
    return lax.dot_general(a, b, (((1,), (1,)), ((), ())), preferred_element_type=F32)


def _fold(x, op):
    return op(x.reshape(x.shape[0] // SUBLANES, SUBLANES, x.shape[1]), axis=0)


def _proj_gla_kernel(x_ref, g_ref, w_ref, q_ref, k_ref, v_ref, a_ref, z_ref):
    hb = _rms_rows(x_ref[...], g_ref[...]).astype(BF16)

    def seg(a, b):
        return jnp.dot(hb, w_ref[:, a:b], preferred_element_type=F32)

    q_ref[...] = seg(0, 256).astype(BF16)
    k_ref[...] = seg(256, 512).astype(BF16)
    v_ref[...] = seg(512, 1024).astype(BF16)
    a_ref[...] = seg(1024, 1152).astype(BF16)
    z_ref[...] = seg(1152, 1664)


def _head_rms(y, g):
    outs = []
    for h in range(4):
        yh = y[:, h * LANES:(h + 1) * LANES]
        ms = jnp.mean(yh * yh, axis=-1, keepdims=True)
        outs.append((yh * lax.rsqrt(ms + EPS)) * g)
    return outs


def _store_values(vt_ref, blk, v_t):
    ones = jnp.ones((PACK, v_t.shape[1]), BF16)
    for h in range(4):
        vt_ref[blk, h * VROWS:h * VROWS + HEAD_W, :] = v_t[h * HEAD_W:(h + 1) * HEAD_W, :]
        vt_ref[blk, h * VROWS + HEAD_W:(h + 1) * VROWS, :] = ones


def _proj_dsa_kernel(x_ref, g_ref, wt_ref, w_ref, qg_ref, kg_ref,
                     qt_ref, k_ref, vt_ref, iqt_ref, ik_ref, iwt_ref, z_ref):
    hb = _rms_rows(x_ref[...], g_ref[...]).astype(BF16)
    nblk = hb.shape[0] // TBLK

    def seg(a, b):
        return jnp.dot(hb, w_ref[:, a:b], preferred_element_type=F32)

    def seg_t(a, b):
        return _nt(wt_ref[a:b, :], hb)

    q_t = seg_t(0, 512)
    for h in range(DSA_HEADS):
        rs = slice(h * DSA_HD, (h + 1) * DSA_HD)
        xh = q_t[rs, :]
        ms = jnp.mean(xh * xh, axis=0, keepdims=True)
        xn = (((xh * lax.rsqrt(ms + EPS)) * qg_ref[...]) * (DSA_HD ** -0.5 * LOG2E)).astype(BF16)
        for blk in range(nblk):
            qt_ref[blk, rs, :] = xn[:, blk * TBLK:(blk + 1) * TBLK]
    v_t = seg_t(512, 1024).astype(BF16)
    iq_t = seg_t(1024, 1536).astype(BF16)
    iw_t = seg_t(1536, 1552)
    for blk in range(nblk):
        cs = slice(blk * TBLK, (blk + 1) * TBLK)
        _store_values(vt_ref, blk, v_t[:, cs])
        iqt_ref[blk] = iq_t[:, cs]
        iwt_ref[blk] = iw_t[0:IDX_HEADS, cs]
    for h, kh in enumerate(_head_rms(seg(0, 512), kg_ref[...])):
        k_ref[:, h * LANES:(h + 1) * LANES] = kh.astype(BF16)
    ik_ref[...] = seg(512, 640).astype(BF16)
    z_ref[...] = seg(640, 1152)


def _half_rms(y, g2):
    lo = lax.broadcasted_iota(I32, (y.shape[0], LANES), 1) < DIFF_HD
    outs = []
    for h in range(4):
        yh = y[:, h * LANES:(h + 1) * LANES]
        sq = yh * yh
        s_lo = jnp.sum(jnp.where(lo, sq, 0.0), axis=-1, keepdims=True)
        s_hi = jnp.sum(jnp.where(lo, 0.0, sq), axis=-1, keepdims=True)
        ms = jnp.where(lo, s_lo, s_hi) * (1.0 / DIFF_HD)
        outs.append((yh * lax.rsqrt(ms + EPS)) * g2)
    return outs


def _proj_diff_kernel(x_ref, g_ref, wt_ref, w_ref, qg_ref, kg_ref,
                      q1_ref, q2_ref, k_ref, vt_ref, z_ref):
    hb = _rms_rows(x_ref[...], g_ref[...]).astype(BF16)
    nblk = hb.shape[0] // TBLK
    q_t = _nt(wt_ref[0:512, :], hb)
    v_t = _nt(wt_ref[512:1024, :], hb)
    zeros = jnp.zeros((DIFF_HD, TBLK), BF16)
    for j in range(2 * DIFF_HEADS):
        rs = slice(j * DIFF_HD, (j + 1) * DIFF_HD)
        xj = q_t[rs, :]
        ms = jnp.mean(xj * xj, axis=0, keepdims=True)
        xn = (((xj * lax.rsqrt(ms + EPS)) * qg_ref[...]) * (DIFF_HD ** -0.5 * LOG2E)).astype(BF16)
        own, other = (q1_ref, q2_ref) if j % 2 == 0 else (q2_ref, q1_ref)
        for blk in range(nblk):
            own[blk, rs, :] = xn[:, blk * TBLK:(blk + 1) * TBLK]
            other[blk, rs, :] = zeros
    for blk in range(nblk):
        _store_values(vt_ref, blk, v_t[:, blk * TBLK:(blk + 1) * TBLK].astype(BF16))
    ks = _half_rms(jnp.dot(hb, w_ref[:, 0:512], preferred_element_type=F32), kg_ref[...])
    for h, kh in enumerate(ks):
        k_ref[:, h * LANES:(h + 1) * LANES] = kh.astype(BF16)
    z_ref[...] = jnp.dot(hb, w_ref[:, 512:1024], preferred_element_type=F32)


def _row_spec(width, rows=PROJ_ROWS):
    return pl.BlockSpec((rows, width), lambda i: (i, 0))


def _full_spec(shape):
    nd = len(shape)
    return pl.BlockSpec(shape, lambda i: (0,) * nd)


def _row_out(n, width, dtype):
    return jax.ShapeDtypeStruct((n, width), dtype), _row_spec(width)


def _fm_out(n, feat, dtype):
    return (jax.ShapeDtypeStruct((n // TBLK, feat, TBLK), dtype),
            pl.BlockSpec((PROJ_ROWS // TBLK, feat, TBLK), lambda i: (i, 0, 0)))


def _proj_call(kernel, name, x2, g, consts, outs):
    n = x2.shape[0]
    in_specs = [_row_spec(D_MODEL), _full_spec(g.shape)] + [_full_spec(e.shape) for e in consts]
    return pl.pallas_call(
        kernel,
        grid=(n // PROJ_ROWS,),
        in_specs=in_specs,
        out_specs=[spec for _, spec in outs],
        out_shape=[shape for shape, _ in outs],
        compiler_params=_cparams(("parallel",)),
        name=name,
    )(x2, g, *consts)


def _gla_kernel(q_ref, k_ref, v_ref, a_ref, z_ref, wa_ref, ba_ref, ng_ref, o_ref, st_ref):
    nbatch, rows = q_ref.shape[0], q_ref.shape[1]
    nchunk = rows // CHUNK

    @pl.when(pl.program_id(0) == 0)
    def _():
        st_ref[...] = jnp.zeros_like(st_ref)

    pos = lax.broadcasted_iota(I32, (rows, 1), 0) & (CHUNK - 1)
    lo = lax.broadcasted_iota(I32, (CHUNK, LANES), 1) < GLA_DK
    cums = []
    for bb in range(nbatch):
        pre = jnp.dot(a_ref[bb], wa_ref[...], preferred_element_type=F32) + ba_ref[...]
        cum = -(jnp.maximum(-pre, 0.0) + jnp.log1p(jnp.exp(-jnp.abs(pre)))) / GLA_TAU
        s = 1
        while s < CHUNK:
            cum = cum + jnp.where(pos >= s, pltpu.roll(cum, s, axis=0), 0.0)
            s *= 2
        cums.append(cum)

    for c in range(nchunk):
        r0 = c * CHUNK
        for bb in range(nbatch):
            cum = cums[bb]
            total = cum[r0 + CHUNK - 1:r0 + CHUNK, :]
            decay_to_end = jnp.exp(total - cum[r0:r0 + CHUNK, :])
            k_dec = (k_ref[bb, r0:r0 + CHUNK, :].astype(F32) * decay_to_end).astype(BF16)
            a_tot = jnp.exp(total)
            q_c = q_ref[bb, r0:r0 + CHUNK, :].astype(F32) * (GLA_DK ** -0.5)
            for h in range(GLA_HEADS):
                j = h // 2
                pair = slice(j * LANES, (j + 1) * LANES)
                hs = slice(h * GLA_DV, (h + 1) * GLA_DV)
                u_t = lax.dot_general(v_ref[bb, r0:r0 + CHUNK, hs], k_dec[:, pair],
                                      (((0,), (0,)), ((), ())),
                                      preferred_element_type=F32)
                st = st_ref[bb * GLA_HEADS + h] * a_tot[:, pair] + u_t
                st_ref[bb * GLA_HEADS + h] = st
                keep = lo if h % 2 == 0 else jnp.logical_not(lo)
                q_h = jnp.where(keep, q_c[:, pair], 0.0).astype(BF16)
                o = _nt(q_h, st.astype(BF16))
                ms = jnp.mean(o * o, axis=-1, keepdims=True)
                o = (o * lax.rsqrt(ms + EPS)) * ng_ref[...]
                zh = z_ref[bb, r0:r0 + CHUNK, hs]
                o_ref[bb, r0:r0 + CHUNK, hs] = (o * _silu(zh)).astype(BF16)


def _gla_call(q, k, v, a, z, wa, ba, ng, b, t):
    nb = t // GLA_ROWS

    def rs(width):
        return pl.BlockSpec((b, GLA_ROWS, width), lambda i: (0, i, 0))

    def fs(shape):
        return pl.BlockSpec(shape, lambda i: (0,) * len(shape))

    def by_batch(x):
        return x.reshape(b, t, x.shape[-1])

    out = pl.pallas_call(
        _gla_kernel,
        grid=(nb,),
        in_specs=[rs(256), rs(256), rs(512), rs(128), rs(512),
                  fs(wa.shape), fs(ba.shape), fs(ng.shape)],
        out_specs=rs(512),
        out_shape=jax.ShapeDtypeStruct((b, t, BR_W), BF16),
        scratch_shapes=[pltpu.VMEM((b * GLA_HEADS, GLA_DV, LANES), F32)],
        compiler_params=_cparams(("arbitrary",)),
        name="gla_mixer",
    )(by_batch(q), by_batch(k), by_batch(v), by_batch(a), by_batch(z), wa, ba, ng)
    return out.reshape(b * t, BR_W)


def _softmax_unit(s_ref, p_ref, m_ref, i, allowed=None):
    nk = s_ref.shape[1]

    def piece(r):
        s = s_ref[i, r:r + SUB, :]
        return s if allowed is None else jnp.where(allowed(r), s, NEG)

    mx = None
    for r in range(0, nk, SUB):
        part = _fold(piece(r), jnp.max)
        mx = part if mx is None else jnp.maximum(mx, part)
    m_old = m_ref[i]
    m_new = jnp.maximum(m_old, jnp.max(mx, axis=0, keepdims=True))
    for r in range(0, nk, SUB):
        p_ref[i, r:r + SUB, :] = jnp.exp2((piece(r) - m_new).astype(BF16))
    m_ref[i] = m_new
    return jnp.exp2(m_old - m_new)


def _dsa_kernel(qt_ref, iqt_ref, iwt_ref, z_ref, qg_ref, kg_ref, k_ref, vt_ref, ik_ref, o_ref,
                sc_ref, hi_ref, lo_ref, tau_ref, nge_ref, ok_ref, iqm_ref, d_ref, acc_ref, m_ref,
                s0_ref, s1_ref, p_ref, *, topk, idx_bits):
    nq = qt_ref.shape[1]
    kb_sz = sc_ref.shape[1]
    nhalf = kb_sz // TBLK
    qi = pl.program_id(1)
    q0 = qi * nq
    nkb = (q0 + nq + kb_sz - 1) >> (kb_sz.bit_length() - 1)

    q_end = q0 + ((lax.broadcasted_iota(I32, (1, nq), 1) >> CHUNK_SHIFT) + 1) * CHUNK
    sub_iota = lax.broadcasted_iota(I32, (SUB, nq), 0)

    upper = lax.broadcasted_iota(I32, (LANES, nq), 0) >= IDX_HD
    for h in range(IDX_HEADS):
        pair = iqt_ref[(h // 2) * LANES:(h // 2 + 1) * LANES, :].astype(F32)
        keep = upper if h % 2 else jnp.logical_not(upper)
        iqm_ref[h] = jnp.where(keep, pair, 0.0).astype(BF16)
    wc = iwt_ref[...] * (IDX_HEADS ** -0.5 * IDX_HD ** -0.5)

    assert nhalf == 2
    n_sub = nkb * nhalf

    def head_dots(j, half):
        ikb = ik_ref[pl.ds(pl.multiple_of(j * TBLK, TBLK), TBLK), :]
        for h in range(IDX_HEADS):
            d_ref[half * IDX_HEADS + h] = jnp.dot(ikb, iqm_ref[h], preferred_element_type=F32)

    def combine(kb, half):
        d0 = half * IDX_HEADS
        for r in range(0, TBLK, SUB):
            s = jnp.zeros((SUB, nq), F32)
            for h in range(IDX_HEADS):
                s = s + jnp.maximum(d_ref[d0 + h, r:r + SUB, :], 0.0) * wc[h:h + 1, :]
            store_scores(kb, half * TBLK + r, s, None)

    def score_pair(t, carry):
        head_dots(4 * t + 1, 1)
        combine(2 * t, 0)
        head_dots(4 * t + 2, 0)
        combine(2 * t, 1)
        head_dots(4 * t + 3, 1)
        combine(2 * t + 1, 0)
        head_dots(jnp.minimum(4 * t + 4, n_sub - 1), 0)
        combine(2 * t + 1, 1)
        return carry

    def store_scores(kb, r, s, visible):
        rs = slice(r, r + SUB)
        bits = lax.bitcast_convert_type(s, I32)
        sign = bits >> 31
        key = ((bits & 0x7FFFFFFF) ^ sign) - sign
        if visible is not None:
            s = jnp.where(visible, s, -jnp.inf)
            key = jnp.where(visible, key, INT_MIN)
        sc_ref[kb, rs, :] = s
        hi_ref[kb, rs, :] = (key >> 16).astype(I16)
        lo_ref[kb, rs, :] = ((key & 0xFFFF) + I16_MIN).astype(I16)

    head_dots(jnp.int32(0), 0)
    lax.fori_loop(0, nkb >> 1, score_pair, 0)

    @pl.when((nkb & 1) == 1)
    def _():
        head_dots(n_sub - 1, 1)
        combine(nkb - 1, 0)
        combine(nkb - 1, 1)

    for r in range(0, kb_sz, SUB):
        pos = (nkb - 1) * kb_sz + r + sub_iota
        store_scores(nkb - 1, r, sc_ref[nkb - 1, r:r + SUB, :], pos < q_end)


    def count(*preds):
        def body(kb, parts):
            parts = list(parts)
            for r in range(0, kb_sz, SUB):
                sv = sc_ref[kb, r:r + SUB, :]
                for n, pred in enumerate(preds):
                    hit = pred(sv, kb * kb_sz + r)
                    parts[n] = parts[n] + _fold(jnp.where(hit, 1.0, 0.0), jnp.sum)
            return tuple(parts)
        parts = lax.fori_loop(0, nkb, body, (jnp.zeros((SUBLANES, nq), F32),) * len(preds))
        sums = [jnp.sum(part, axis=0, keepdims=True) for part in parts]
        return sums[0] if len(sums) == 1 else sums

    n_acc = 4
    piece = n_acc * PACK * 2
    one16, zero16 = jnp.ones((), I16), jnp.zeros((), I16)

    def count16(plane_ref, pred):
        def block(kb, accs):
            accs = list(accs)
            for r in range(0, kb_sz, piece):
                ones = jnp.where(pred(plane_ref[kb, r:r + piece, :]), one16, zero16)
                for j in range(piece // PACK):
                    accs[j % n_acc] = accs[j % n_acc] + ones[j * PACK:(j + 1) * PACK, :]
            return tuple(accs)

        def pair(t, accs):
            return block(2 * t + 1, block(2 * t, accs))

        accs = lax.fori_loop(0, nkb >> 1, pair, (jnp.zeros((PACK, nq), I16),) * n_acc)
        accs = lax.fori_loop(nkb - (nkb & 1), nkb, block, accs)
        tot = (accs[0] + accs[1]) + (accs[2] + accs[3])
        return jnp.sum(tot.astype(I32).astype(F32), axis=0, keepdims=True)

    def search16(plane_ref, want, c_start):
        def bit_body(i, carry):
            res, cres = carry
            cand = res | lax.shift_left(jnp.int32(1), 15 - i)
            cand16 = (cand + I16_MIN).astype(I16)
            cnt = count16(plane_ref, lambda kv: kv >= cand16)
            take = cnt >= want
            return jnp.where(take, cand, res), jnp.where(take, cnt, cres)
        return lax.fori_loop(0, 16, bit_body, (jnp.zeros((1, nq), I32), c_start))

    zero_f = jnp.zeros((1, nq), F32)
    hi_u, c_ge_hi = search16(hi_ref, jnp.full((1, nq), float(topk), F32), zero_f)
    hi16 = (hi_u + I16_MIN).astype(I16)
    c_gt_hi = count16(hi_ref, lambda kv: kv > hi16)
    low_min = jnp.full((), I16_MIN, I16)

    def bucket_body(kb, carry):
        for r in range(0, kb_sz, piece):
            rs = slice(r, r + piece)
            lo_ref[kb, rs, :] = jnp.where(hi_ref[kb, rs, :] == hi16, lo_ref[kb, rs, :], low_min)
        return carry

    lax.fori_loop(0, nkb, bucket_body, 0)
    lo_u, _ = search16(lo_ref, topk - c_gt_hi, c_ge_hi - c_gt_hi)
    thr_key = ((hi_u + I16_MIN) << 16) | lo_u

    def key_to_score(key):
        bits = jnp.where(key < 0, (-key) | INT_MIN, key)
        return lax.bitcast_convert_type(bits, F32)

    few = q_end <= topk
    lowest = float(jnp.finfo(F32).min)
    tau0 = jnp.where(few, lowest, key_to_score(thr_key))

    n_ge0 = count(lambda sv, _: sv >= tau0)
    separates = jnp.logical_or(few, n_ge0 == topk)
    tau_ref[...] = tau0
    nge_ref[...] = n_ge0
    ok_ref[...] = jnp.where(separates, 1.0, 0.0)

    @pl.when(jnp.min(ok_ref[...]) < 1.0)
    def _():
        n_gt0 = count(lambda sv, _: sv > tau0)
        tied = jnp.logical_and(n_ge0 > topk, n_gt0 < topk)
        ok_ref[...] = jnp.where(jnp.logical_or(separates, tied), 1.0, 0.0)

    @pl.when(jnp.min(ok_ref[...]) < 1.0)
    def _():
        def bit_body(i, carry):
            res, cres = carry
            cand = res | lax.shift_left(jnp.int32(1), 31 - i)
            cand_f = key_to_score(cand ^ INT_MIN)
            cnt = count(lambda sv, _: sv >= cand_f)
            take = cnt >= topk
            return jnp.where(take, cand, res), jnp.where(take, cnt, cres)

        res, cres = lax.fori_loop(0, 32, bit_body,
                                  (jnp.zeros((1, nq), I32), jnp.zeros((1, nq), F32)))
        tau_ref[...] = jnp.where(few, lowest, key_to_score(res ^ INT_MIN))
        nge_ref[...] = cres

    tau = tau_ref[...]

    excess = jnp.where(jnp.logical_and(nge_ref[...] > topk, jnp.logical_not(few)), 1.0, 0.0)

    @pl.when(jnp.max(excess) > 0.0)
    def _():
        need = topk - count(lambda sv, _: sv > tau)
        past = jnp.iinfo(I16).max

        def tied_body(kb, carry):
            for r in range(0, kb_sz, SUB):
                idx = kb * kb_sz + r + sub_iota
                tied_idx = jnp.where(sc_ref[kb, r:r + SUB, :] == tau, idx, past)
                lo_ref[kb, r:r + SUB, :] = tied_idx.astype(I16)
            return carry

        lax.fori_loop(0, nkb, tied_body, 0)

        def idx_body(i, p):
            cand = p | lax.shift_left(jnp.int32(1), idx_bits - 1 - i)
            cand16 = cand.astype(I16)
            below = count16(lo_ref, lambda kv: kv < cand16)
            return jnp.where(below <= need - 1.0, cand, p)

        last = lax.fori_loop(0, idx_bits, idx_body, jnp.zeros((1, nq), I32))
        excess_b = excess > 0.0

        def demote_body(kb, carry):
            for r in range(0, kb_sz, SUB):
                sv = sc_ref[kb, r:r + SUB, :]
                drop = jnp.logical_and(jnp.logical_and(sv == tau, excess_b),
                                       kb * kb_sz + r + sub_iota > last)
                sc_ref[kb, r:r + SUB, :] = jnp.where(drop, -jnp.inf, sv)
            return carry

        lax.fori_loop(0, nkb, demote_body, 0)

    acc_ref[...] = jnp.zeros_like(acc_ref)
    m_ref[...] = jnp.full_like(m_ref, NEG)

    nblk = (q0 + nq) >> (TBLK.bit_length() - 1)
    half_shift = nhalf.bit_length() - 1

    def logits(j, s_ref):
        kstart = pl.multiple_of(j * TBLK, TBLK)
        krows = pl.ds(pl.multiple_of((j & (nhalf - 1)) * TBLK, TBLK), TBLK)
        for h in range(DSA_HEADS):
            hs = slice(h * DSA_HD, (h + 1) * DSA_HD)
            s_t = jnp.dot(k_ref[pl.ds(kstart, TBLK), hs], qt_ref[hs, :],
                          preferred_element_type=F32)
            sel = sc_ref[j >> half_shift, krows, :] >= tau
            s_ref[h] = jnp.where(sel, s_t, NEG)

    def consume(j, s_ref):
        for h in range(DSA_HEADS):
            alpha = _softmax_unit(s_ref, p_ref, m_ref, h)
            acc_ref[h] = alpha * acc_ref[h] + jnp.dot(
                vt_ref[j, h * VROWS:(h + 1) * VROWS, :], p_ref[h],
                preferred_element_type=F32)

    def pair_body(t, carry):
        logits(2 * t + 1, s1_ref)
        consume(2 * t, s0_ref)
        logits(jnp.minimum(2 * t + 2, nblk - 1), s0_ref)
        consume(2 * t + 1, s1_ref)
        return carry

    def unshifted_probs(j, slot):
        kstart = pl.multiple_of(j * TBLK, TBLK)
        krows = pl.ds(pl.multiple_of((j & (nhalf - 1)) * TBLK, TBLK), TBLK)
        for h in range(DSA_HEADS):
            hs = slice(h * DSA_HD, (h + 1) * DSA_HD)
            s_t = jnp.dot(k_ref[pl.ds(kstart, TBLK), hs], qt_ref[hs, :],
                          preferred_element_type=F32)
            sel = sc_ref[j >> half_shift, krows, :] >= tau
            p_ref[slot * DSA_HEADS + h] = jnp.exp2(jnp.where(sel, s_t, NEG)).astype(BF16)

    def weighted_values(j, slot, h):
        return jnp.dot(vt_ref[j, h * VROWS:(h + 1) * VROWS, :], p_ref[slot * DSA_HEADS + h],
                       preferred_element_type=F32)

    def unshifted_blocks(j0, n):
        for s in range(n):
            unshifted_probs(j0 + s, s)
        for h in range(DSA_HEADS):
            total = weighted_values(j0, 0, h)
            for s in range(1, n):
                total = total + weighted_values(j0 + s, s, h)
            acc_ref[h] += total

    def unshifted_group(t, carry):
        unshifted_blocks(ATT_GROUP * t, ATT_GROUP)
        return carry

    bound = (DSA_HD ** 0.5 * LOG2E * NORM_SLACK) * jnp.max(jnp.abs(qg_ref[...])) * jnp.max(
        jnp.abs(kg_ref[...]))
    small = bound <= MAX_UNSHIFTED_LOG2

    @pl.when(small)
    def _():
        ngroup = nblk >> (ATT_GROUP.bit_length() - 1)
        lax.fori_loop(0, ngroup, unshifted_group, 0)
        done = ngroup * ATT_GROUP

        @pl.when((nblk & 2) != 0)
        def _():
            unshifted_blocks(done, 2)

        @pl.when((nblk & 1) != 0)
        def _():
            unshifted_blocks(nblk - 1, 1)

    @pl.when(jnp.logical_not(small))
    def _():
        logits(jnp.int32(0), s0_ref)
        lax.fori_loop(0, nblk >> 1, pair_body, 0)

        @pl.when((nblk & 1) == 1)
        def _():
            consume(nblk - 1, s0_ref)

    for h in range(DSA_HEADS):
        hs = slice(h * DSA_HD, (h + 1) * DSA_HD)
        o = (acc_ref[h, 0:HEAD_W, :] / acc_ref[h, HEAD_W:HEAD_W + 1, :]).T
        o_ref[:, hs] = (o * _silu(z_ref[:, hs])).astype(BF16)


def _dsa_call(qt, iqt, iwt, z, qg, kg, k, vt, ik, b, t):
    nq = t // TBLK

    def fs(shape):
        return pl.BlockSpec(shape, lambda bi, i: (0,) * len(shape))

    topk = min(TOPK_MAX, t // 4)
    idx_bits = max(1, (t - 1).bit_length())
    assert t < 2 ** 15, "key indices are held in an int16 plane by the tie path"

    def fm_q(feat):
        return pl.BlockSpec((None, feat, TBLK), lambda bi, i: (bi * nq + i, 0, 0))

    def rows_q(width):
        return pl.BlockSpec((TBLK, width), lambda bi, i: (bi * nq + i, 0))

    def rows_b(width):
        return _resident((t, width), lambda bi, i: (bi, 0))

    kernel = functools.partial(_dsa_kernel, topk=topk, idx_bits=idx_bits)
    return pl.pallas_call(
        kernel,
        grid=(b, nq),
        in_specs=[fm_q(512), fm_q(512), fm_q(IDX_HEADS), rows_q(512), fs(qg.shape), fs(kg.shape),
                  rows_b(512),
                  _resident((nq, 4 * VROWS, TBLK), lambda bi, i: (bi, 0, 0)), rows_b(LANES)],
        out_specs=rows_q(512),
        out_shape=jax.ShapeDtypeStruct((b * t, BR_W), BF16),
        scratch_shapes=[
            pltpu.VMEM((t // DSA_KB, DSA_KB, TBLK), F32),
            pltpu.VMEM((t // DSA_KB, DSA_KB, TBLK), I16),
            pltpu.VMEM((t // DSA_KB, DSA_KB, TBLK), I16),
            pltpu.VMEM((1, TBLK), F32),
            pltpu.VMEM((1, TBLK), F32),
            pltpu.VMEM((1, TBLK), F32),
            pltpu.VMEM((IDX_HEADS, LANES, TBLK), BF16),
            pltpu.VMEM((DSA_KB // TBLK * IDX_HEADS, TBLK, TBLK), F32),
            pltpu.VMEM((DSA_HEADS, VROWS, TBLK), F32),
            pltpu.VMEM((DSA_HEADS, 1, TBLK), F32),
            pltpu.VMEM((DSA_HEADS, TBLK, TBLK), F32),
            pltpu.VMEM((DSA_HEADS, TBLK, TBLK), F32),
            pltpu.VMEM((ATT_GROUP * DSA_HEADS, TBLK, TBLK), BF16),
        ],
        compiler_params=_cparams(("parallel", "arbitrary")),
        name="dsa_mixer",
    )(qt, iqt, iwt, z, qg, kg, k, vt, ik)


def _diff_kernel(q1_ref, q2_ref, z_ref, lq1_ref, lk1_ref, lq2_ref, lk2_ref, qg_ref, kg_ref,
                 k_ref, vt_ref, o_ref, acc_ref, m_ref, s0_ref, s1_ref, p_ref, *, lambda_init):
    nq = q1_ref.shape[1]
    qi = pl.program_id(1)
    lam = (jnp.exp(jnp.sum(lq1_ref[...] * lk1_ref[...], axis=-1, keepdims=True))
           - jnp.exp(jnp.sum(lq2_ref[...] * lk2_ref[...], axis=-1, keepdims=True))
           + lambda_init)

    acc_ref[...] = jnp.zeros_like(acc_ref)
    m_ref[...] = jnp.full_like(m_ref, NEG)

    q_end = ((lax.broadcasted_iota(I32, (1, nq), 1) >> CHUNK_SHIFT) + 1) * CHUNK
    sub_iota = lax.broadcasted_iota(I32, (SUB, nq), 0)

    def logits(kb, s_ref):
        rows = pl.ds(pl.multiple_of(kb * nq, nq), nq)
        for h in range(DIFF_HEADS):
            hs = slice(h * LANES, (h + 1) * LANES)
            k_h = k_ref[rows, hs]
            for c, q_ref in enumerate((q1_ref, q2_ref)):
                s_ref[2 * h + c] = jnp.dot(k_h, q_ref[hs, :], preferred_element_type=F32)

    def consume(kb, s_ref, allowed=None):
        for i in range(2 * DIFF_HEADS):
            vs = slice((i // 2) * VROWS, (i // 2 + 1) * VROWS)
            alpha = _softmax_unit(s_ref, p_ref, m_ref, i, allowed)
            acc_ref[i] = alpha * acc_ref[i] + jnp.dot(vt_ref[kb, vs, :], p_ref[i],
                                                      preferred_element_type=F32)

    def pair_body(t, carry):
        logits(2 * t + 1, s1_ref)
        consume(2 * t, s0_ref)
        logits(2 * t + 2, s0_ref)
        consume(2 * t + 1, s1_ref)
        return carry

    def chunk_mask(r):
        return r + sub_iota < q_end

    units = 2 * DIFF_HEADS

    def unshifted_probs(kb, slot, diagonal):
        rows = pl.ds(pl.multiple_of(kb * nq, nq), nq)
        for h in range(DIFF_HEADS):
            hs = slice(h * LANES, (h + 1) * LANES)
            k_h = k_ref[rows, hs]
            for c, q_ref in enumerate((q1_ref, q2_ref)):
                s_t = jnp.dot(k_h, q_ref[hs, :], preferred_element_type=F32)
                if diagonal:
                    s_t = jnp.where(lax.broadcasted_iota(I32, (nq, nq), 0) < q_end, s_t, NEG)
                p_ref[slot * units + 2 * h + c] = jnp.exp2(s_t).astype(BF16)

    def weighted_values(kb, slot, i):
        vs = slice((i // 2) * VROWS, (i // 2 + 1) * VROWS)
        return jnp.dot(vt_ref[kb, vs, :], p_ref[slot * units + i], preferred_element_type=F32)

    def unshifted_blocks(kb0, n, diagonal_last):
        for s in range(n):
            unshifted_probs(kb0 + s, s, diagonal_last and s == n - 1)
        for i in range(units):
            total = weighted_values(kb0, 0, i)
            for s in range(1, n):
                total = total + weighted_values(kb0 + s, s, i)
            acc_ref[i] += total

    def unshifted_group(t, carry):
        unshifted_blocks(ATT_GROUP * t, ATT_GROUP, False)
        return carry

    bound = (DIFF_HD ** 0.5 * LOG2E * NORM_SLACK) * jnp.max(jnp.abs(qg_ref[...])) * jnp.max(
        jnp.abs(kg_ref[...]))
    small = bound <= MAX_UNSHIFTED_LOG2

    @pl.when(small)
    def _():
        ngroup = qi >> (ATT_GROUP.bit_length() - 1)
        lax.fori_loop(0, ngroup, unshifted_group, 0)
        done = ngroup * ATT_GROUP

        @pl.when((qi & 2) != 0)
        def _():
            unshifted_blocks(done, 2, False)

        @pl.when((qi & 1) != 0)
        def _():
            unshifted_blocks(qi - 1, 2, True)

        @pl.when((qi & 1) == 0)
        def _():
            unshifted_blocks(qi, 1, True)

    @pl.when(jnp.logical_not(small))
    def _():
        logits(jnp.int32(0), s0_ref)
        lax.fori_loop(0, qi >> 1, pair_body, 0)

        @pl.when((qi & 1) == 0)
        def _():
            consume(qi, s0_ref, chunk_mask)

        @pl.when((qi & 1) == 1)
        def _():
            logits(qi, s1_ref)
            consume(qi - 1, s0_ref)
            consume(qi, s1_ref, chunk_mask)

    def normalised(i):
        return acc_ref[i, 0:HEAD_W, :] / acc_ref[i, HEAD_W:HEAD_W + 1, :]

    for h in range(DIFF_HEADS):
        hs = slice(h * LANES, (h + 1) * LANES)
        o_t = normalised(2 * h) - lam * normalised(2 * h + 1)
        ms = jnp.mean(o_t * o_t, axis=0, keepdims=True)
        o = ((o_t * lax.rsqrt(ms + EPS)) * (1.0 - lambda_init)).T
        o_ref[:, hs] = (o * _silu(z_ref[:, hs])).astype(BF16)


def _diff_call(q1, q2, z, lq1, lk1, lq2, lk2, qg, kg, k, vt, b, t, lambda_init):
    nq = t // TBLK

    def fm_q():
        return pl.BlockSpec((None, 512, TBLK), lambda bi, i: (bi * nq + i, 0, 0))

    def fs(shape):
        return pl.BlockSpec(shape, lambda bi, i: (0,) * len(shape))

    kernel = functools.partial(_diff_kernel, lambda_init=lambda_init)
    return pl.pallas_call(
        kernel,
        grid=(b, nq),
        in_specs=[fm_q(), fm_q(),
                  pl.BlockSpec((TBLK, 512), lambda bi, i: (bi * nq + i, 0)),
                  fs(lq1.shape), fs(lk1.shape), fs(lq2.shape), fs(lk2.shape),
                  fs(qg.shape), fs(kg.shape),
                  _resident((t, 512), lambda bi, i: (bi, 0)),
                  _resident((nq, 4 * VROWS, TBLK), lambda bi, i: (bi, 0, 0))],
        out_specs=pl.BlockSpec((TBLK, 512), lambda bi, i: (bi * nq + i, 0)),
        out_shape=jax.ShapeDtypeStruct((b * t, BR_W), BF16),
        scratch_shapes=[
            pltpu.VMEM((2 * DIFF_HEADS, VROWS, TBLK), F32),
            pltpu.VMEM((2 * DIFF_HEADS, 1, TBLK), F32),
            pltpu.VMEM((2 * DIFF_HEADS, TBLK, TBLK), F32),
            pltpu.VMEM((2 * DIFF_HEADS, TBLK, TBLK), F32),
            pltpu.VMEM((ATT_GROUP * 2 * DIFF_HEADS, TBLK, TBLK), BF16),
        ],
        compiler_params=_cparams(("parallel", "arbitrary")),
        name="diff_mixer",
    )(q1, q2, z, lq1, lk1, lq2, lk2, qg, kg, k, vt)


def _out_kernel(x_ref, g_ref, wg_ref, ya_ref, yb_ref, yc_ref, wbr_ref, wo_ref, o_ref):
    x = x_ref[...]
    hb = _rms_rows(x, g_ref[...]).astype(BF16)
    merged = jnp.zeros(x.shape, F32)
    for i, y_ref in enumerate((ya_ref, yb_ref, yc_ref)):
        gate = _sigmoid(jnp.dot(hb, wg_ref[:, i * D_MODEL:(i + 1) * D_MODEL],
                                preferred_element_type=F32))
        merged = merged + gate * jnp.dot(y_ref[...], wbr_ref[i], preferred_element_type=F32)
    o_ref[...] = x + jnp.dot(merged.astype(BF16), wo_ref[...], preferred_element_type=F32)


def _out_call(x2, g, wg, ya, yb, yc, wbr, wo):
    n = x2.shape[0]
    return pl.pallas_call(
        _out_kernel,
        grid=(n // PROJ_ROWS,),
        in_specs=[_row_spec(D_MODEL), _full_spec(g.shape), _full_spec(wg.shape),
                  _row_spec(BR_W), _row_spec(BR_W), _row_spec(BR_W),
                  _full_spec(wbr.shape), _full_spec(wo.shape)],
        out_specs=_row_spec(D_MODEL),
        out_shape=jax.ShapeDtypeStruct((n, D_MODEL), F32),
        compiler_params=_cparams(("parallel",)),
        name="merge_out",
    )(x2, g, wg, ya, yb, yc, wbr, wo)


def _pad_cols(w, width):
    return jnp.pad(w, ((0, 0), (0, width - w.shape[1])))


def kernel(x, norm_g, w_in, gla_wa2, gla_ba, gla_norm_g, dsa_qn_g, dsa_kn_g, diff_qn_g,
           diff_kn_g, diff_lq1, diff_lk1, diff_lq2, diff_lk2, w_br, w_out):
    b, t, d = x.shape
    depth = w_in.shape[0]
    assert d == D_MODEL and t % max(GLA_ROWS, DSA_KB, TBLK) == 0 and (b * t) % PROJ_ROWS == 0
    n = b * t
    x2 = x.reshape(n, d)

    for l in range(depth):
        w = w_in[l]
        seg = [w[:, OFFS[i]:OFFS[i + 1]] for i in range(len(SIZES))]
        (gq, gk, gv, ga, gz, bq, bk, bv, iq, ik, iw, bz, cq, ck, cv, cz, gate) = seg
        w_gla = jnp.concatenate([gq, gk, gv, _pad_cols(ga, LANES), gz], axis=1).astype(BF16)
        wt_dsa = jnp.concatenate([bq, bv, iq, _pad_cols(iw, 16)], axis=1).T.astype(BF16)
        w_dsa = jnp.concatenate([bk, ik, ik, bz], axis=1).astype(BF16)
        wt_diff = jnp.concatenate([cq, cv], axis=1).T.astype(BF16)
        w_diff = jnp.concatenate([ck, cz], axis=1).astype(BF16)
        w_gate = gate.astype(BF16)
        g = norm_g[l].reshape(1, d)

        g_q, g_k, g_v, g_a, g_z = _proj_call(
            _proj_gla_kernel, "proj_gla", x2, g, [w_gla],
            [_row_out(n, 256, BF16), _row_out(n, 256, BF16), _row_out(n, 512, BF16),
             _row_out(n, 128, BF16), _row_out(n, 512, F32)])
        b_qt, b_k, b_vt, i_qt, i_k, i_wt, b_z = _proj_call(
            _proj_dsa_kernel, "proj_dsa", x2, g,
            [wt_dsa, w_dsa, dsa_qn_g[l].reshape(DSA_HD, 1), dsa_kn_g[l].reshape(1, DSA_HD)],
            [_fm_out(n, 512, BF16), _row_out(n, 512, BF16), _fm_out(n, 4 * VROWS, BF16),
             _fm_out(n, 512, BF16), _row_out(n, LANES, BF16), _fm_out(n, IDX_HEADS, F32),
             _row_out(n, 512, F32)])
        c_q1, c_q2, c_k, c_vt, c_z = _proj_call(
            _proj_diff_kernel, "proj_diff", x2, g,
            [wt_diff, w_diff, diff_qn_g[l].reshape(DIFF_HD, 1),
             jnp.tile(diff_kn_g[l], 2).reshape(1, LANES)],
            [_fm_out(n, 512, BF16), _fm_out(n, 512, BF16), _row_out(n, 512, BF16),
             _fm_out(n, 4 * VROWS, BF16), _row_out(n, 512, F32)])

        wa = jnp.pad(gla_wa2[l], ((0, LANES - GLA_RANK), (0, 0))).astype(BF16)
        y_a = _gla_call(g_q, g_k, g_v, g_a, g_z, wa, gla_ba[l].reshape(1, -1),
                        gla_norm_g[l].reshape(1, GLA_DV), b, t)
        y_b = _dsa_call(b_qt, i_qt, i_wt, b_z, dsa_qn_g[l].reshape(1, DSA_HD),
                        dsa_kn_g[l].reshape(1, DSA_HD), b_k, b_vt, i_k, b, t)
        lambda_init = 0.8 - 0.6 * math.exp(-0.3 * l)
        y_c = _diff_call(c_q1, c_q2, c_z, diff_lq1[l].reshape(1, -1), diff_lk1[l].reshape(1, -1),
                         diff_lq2[l].reshape(1, -1), diff_lk2[l].reshape(1, -1),
                         diff_qn_g[l].reshape(1, -1), diff_kn_g[l].reshape(1, -1), c_k, c_vt,
                         b, t, lambda_init)
        x2 = _out_call(x2, g, w_gate, y_a, y_b, y_c, w_br[l].astype(BF16),
                       w_out[l].astype(BF16))
    return x2.reshape(b, t, d)
```
